```python
import math
import jax, jax.numpy as jnp
from jax import lax
import numpy as np

D_MODEL = 2048
BATCH = 4
SEQ = 4096
DEPTH = 1

CHUNK = 64
S5_WIDTH = 512
S5_GROUP = 16
S5_GROUPS = S5_WIDTH // S5_GROUP
S5_STATE = 64
CONV_CHANNELS = 1024
CONV_SPAN = 31
N_BRANCHES = 2
IN_COLS = S5_WIDTH + 2 * CONV_CHANNELS + N_BRANCHES * D_MODEL
N_EXPERT_GROUPS = 8
EXPERTS_PER_GROUP = 8
N_EXPERTS = N_EXPERT_GROUPS * EXPERTS_PER_GROUP
TOP_K = 2
D_EXPERT = 512
MOE_BLOCK = 128
N_MOD = 6
DEEPNORM_ALPHA = (2.0 * DEPTH) ** 0.25
DEEPNORM_BETA = (8.0 * DEPTH) ** -0.25
LN_EPS = 1e-5

kernel_name = "hybrid_s5_conformer_hmoe_deepnorm"


def _layernorm(x, gain=None, bias=None):
    xf = x.astype(jnp.float32)
    mu = jnp.mean(xf, -1, keepdims=True)
    var = jnp.mean(jnp.square(xf - mu), -1, keepdims=True)
    y = (xf - mu) * lax.rsqrt(var + LN_EPS)
    if gain is not None:
        y = y * gain.astype(jnp.float32) + bias.astype(jnp.float32)
    return y.astype(x.dtype)


def _modulate(x, shift, scale):
    return _layernorm(x) * (1 + scale[:, None, :]) + shift[:, None, :]


def _complex_affine_combine(left, right):
    a1r, a1i, b1r, b1i = left
    a2r, a2i, b2r, b2i = right
    return (a2r * a1r - a2i * a1i,
            a2r * a1i + a2i * a1r,
            a2r * b1r - a2i * b1i + b2r,
            a2r * b1i + a2i * b1r + b2i)


def _s5_branch(u, a_re, a_im, log_dt, b_re, b_im, c_re, c_im, d_skip, w_gate):
    f32 = jnp.float32
    bsz, seq, _ = u.shape
    uf = u.astype(f32).reshape(bsz, seq, S5_GROUPS, S5_GROUP)
    dt = jnp.exp(log_dt.astype(f32))[:, None]
    lr = a_re.astype(f32)
    li = a_im.astype(f32)
    mag = jnp.exp(lr * dt)
    ab_re = mag * jnp.cos(li * dt)
    ab_im = mag * jnp.sin(li * dt)
    den = lr * lr + li * li
    nr = ab_re - 1.0
    ni = ab_im
    q_re = ((nr * lr + ni * li) / den)[:, :, None]
    q_im = ((ni * lr - nr * li) / den)[:, :, None]
    br = b_re.astype(f32)
    bi = b_im.astype(f32)
    bb_re = q_re * br - q_im * bi
    bb_im = q_re * bi + q_im * br
    bu_re = jnp.einsum('gph,bsgh->bsgp', bb_re, uf)
    bu_im = jnp.einsum('gph,bsgh->bsgp', bb_im, uf)
    a_seq_re = jnp.broadcast_to(ab_re[None, None], (1, seq, S5_GROUPS, S5_STATE))
    a_seq_im = jnp.broadcast_to(ab_im[None, None], (1, seq, S5_GROUPS, S5_STATE))
    _, _, s_re, s_im = lax.associative_scan(
        _complex_affine_combine, (a_seq_re, a_seq_im, bu_re, bu_im), axis=1)
    y = (jnp.einsum('ghp,bsgp->bsgh', c_re.astype(f32), s_re)
         - jnp.einsum('ghp,bsgp->bsgh', c_im.astype(f32), s_im)
         + d_skip.astype(f32) * uf)
    y = jax.nn.gelu(y.reshape(bsz, seq, S5_WIDTH)).astype(u.dtype)
    return y * jax.nn.sigmoid(y @ w_gate)


def _conv_branch(z, w_dw, b_dw, ln_g, ln_b):
    a, g = jnp.split(z, 2, axis=-1)
    v = a * jax.nn.sigmoid(g)
    v = lax.conv_general_dilated(
        v, w_dw, window_strides=(1,), padding=[(CONV_SPAN - 1, 0)],
        dimension_numbers=('NWC', 'WIO', 'NWC'),
        feature_group_count=CONV_CHANNELS) + b_dw
    v = _layernorm(v, ln_g, ln_b)
    return jax.nn.silu(v)


def _hier_moe(h, w_rg, b_rg, w_re, b_re, w_g, w_u, w_d):
    f32 = jnp.float32
    bsz, seq, d = h.shape
    t = bsz * seq
    hf = h.reshape(t, d)
    g_logits = (hf @ w_rg).astype(f32) + b_rg.astype(f32)
    g_prob = jax.nn.softmax(g_logits, axis=-1)
    grp = jnp.argmax(g_logits, axis=-1).astype(jnp.int32)
    grp_w = jnp.take_along_axis(g_prob, grp[:, None], axis=-1)
    e_logits = ((hf @ w_re).astype(f32) + b_re.astype(f32)).reshape(
        t, N_EXPERT_GROUPS, EXPERTS_PER_GROUP)
    idx = jnp.broadcast_to(grp[:, None, None], (t, 1, EXPERTS_PER_GROUP))
    e_logits = jnp.take_along_axis(e_logits, idx, axis=1)[:, 0]
    top_val, top_loc = lax.top_k(e_logits, TOP_K)
    slot_w = grp_w * jax.nn.softmax(top_val, axis=-1)
    slot_e = grp[:, None] * EXPERTS_PER_GROUP + top_loc.astype(jnp.int32)

    n_slots = t * TOP_K
    flat_e = slot_e.reshape(-1)
    flat_w = slot_w.reshape(-1)
    order = jnp.argsort(flat_e)
    sorted_e = flat_e[order]
    slot_token = (order // TOP_K).astype(jnp.int32)
    counts = jnp.zeros((N_EXPERTS,), jnp.int32).at[flat_e].add(1)
    padded = (counts + MOE_BLOCK - 1) // MOE_BLOCK * MOE_BLOCK
    starts = jnp.cumsum(counts) - counts
    pends = jnp.cumsum(padded)
    pstarts = pends - padded
    dest = pstarts[sorted_e] + jnp.arange(n_slots, dtype=jnp.int32) - starts[sorted_e]
    n_blocks = -(-n_slots // MOE_BLOCK) + N_EXPERTS
    n_rows = n_blocks * MOE_BLOCK
    row_token = jnp.full((n_rows,), t, jnp.int32).at[dest].set(slot_token)
    h_pad = jnp.concatenate([hf, jnp.zeros((1, d), hf.dtype)], axis=0)
    x_rows = h_pad[row_token].reshape(n_blocks, MOE_BLOCK, d)
    block_start = jnp.arange(n_blocks, dtype=jnp.int32) * MOE_BLOCK
    block_expert = jnp.minimum(jnp.searchsorted(pends, block_start, side='right'),
                               N_EXPERTS - 1).astype(jnp.int32)

    def expert_block(args):
        xb, e = args
        return (jax.nn.silu(xb @ w_g[e]) * (xb @ w_u[e])) @ w_d[e]

    y_rows = lax.map(expert_block, (x_rows, block_expert)).reshape(n_rows, d)
    y_slots = y_rows[dest] * flat_w[order][:, None].astype(y_rows.dtype)
    out = jax.ops.segment_sum(y_slots, slot_token, num_segments=t)
    return out.reshape(bsz, seq, d)


def setup_inputs(seed: int = 0) -> dict:
    key = jax.random.key(seed)
    ks = jax.random.split(key, 40)
    f32 = jnp.float32
    L = DEPTH

    def nrm(k, shape, scale):
        return jax.random.normal(k, shape, f32) * scale

    a_im_base = jnp.pi * jnp.arange(S5_STATE, dtype=f32)
    return {
        "x": nrm(ks[0], (BATCH, SEQ, D_MODEL), 1.0),
        "c": nrm(ks[1], (BATCH, D_MODEL), 1.0),
        "w_ada": nrm(ks[2], (L, D_MODEL, N_MOD * D_MODEL), 0.1 * D_MODEL ** -0.5),
        "b_ada": nrm(ks[3], (L, N_MOD * D_MODEL), 0.01),
        "w_in": nrm(ks[4], (L, D_MODEL, IN_COLS), D_MODEL ** -0.5),
        "b_in": nrm(ks[5], (L, IN_COLS), 0.01),
        "s5_a_re": -0.5 * (1.0 + nrm(ks[6], (L, S5_GROUPS, S5_STATE), 0.01)),
        "s5_a_im": a_im_base + nrm(ks[7], (L, S5_GROUPS, S5_STATE), 0.01),
        "s5_log_dt": jax.random.uniform(ks[8], (L, S5_GROUPS), f32,
                                        minval=math.log(1e-3), maxval=math.log(1e-1)),
        "s5_b_re": nrm(ks[9], (L, S5_GROUPS, S5_STATE, S5_GROUP), (2.0 * S5_GROUP) ** -0.5),
        "s5_b_im": nrm(ks[10], (L, S5_GROUPS, S5_STATE, S5_GROUP), (2.0 * S5_GROUP) ** -0.5),
        "s5_c_re": nrm(ks[11], (L, S5_GROUPS, S5_GROUP, S5_STATE), (2.0 * S5_STATE) ** -0.5),
        "s5_c_im": nrm(ks[12], (L, S5_GROUPS, S5_GROUP, S5_STATE), (2.0 * S5_STATE) ** -0.5),
        "s5_d": nrm(ks[13], (L, S5_GROUPS, S5_GROUP), 1.0),
        "w_s5_gate": nrm(ks[14], (L, S5_WIDTH, S5_WIDTH), S5_WIDTH ** -0.5),
        "w_s5_up": nrm(ks[15], (L, S5_WIDTH, D_MODEL), S5_WIDTH ** -0.5),
        "conv_dw": nrm(ks[16], (L, CONV_SPAN, 1, CONV_CHANNELS), CONV_SPAN ** -0.5),
        "conv_dw_b": nrm(ks[17], (L, CONV_CHANNELS), 0.01),
        "conv_ln_g": 1.0 + nrm(ks[18], (L, CONV_CHANNELS), 0.01),
        "conv_ln_b": nrm(ks[19], (L, CONV_CHANNELS), 0.01),
        "w_conv_out": nrm(ks[20], (L, CONV_CHANNELS, D_MODEL), CONV_CHANNELS ** -0.5),
        "w_out": nrm(ks[21], (L, D_MODEL, D_MODEL), DEEPNORM_BETA * D_MODEL ** -0.5),
        "ln1_g": 1.0 + nrm(ks[22], (L, D_MODEL), 0.01),
        "ln1_b": nrm(ks[23], (L, D_MODEL), 0.01),
        "w_route_group": nrm(ks[24], (L, D_MODEL, N_EXPERT_GROUPS), D_MODEL ** -0.5),
        "b_route_group": nrm(ks[25], (L, N_EXPERT_GROUPS), 0.01),
        "w_route_expert": nrm(ks[26], (L, D_MODEL, N_EXPERTS), D_MODEL ** -0.5),
        "b_route_expert": nrm(ks[27], (L, N_EXPERTS), 0.01),
        "w_exp_gate": nrm(ks[28], (L, N_EXPERTS, D_MODEL, D_EXPERT), D_MODEL ** -0.5),
        "w_exp_up": nrm(ks[29], (L, N_EXPERTS, D_MODEL, D_EXPERT), D_MODEL ** -0.5),
        "w_exp_down": nrm(ks[30], (L, N_EXPERTS, D_EXPERT, D_MODEL), DEEPNORM_BETA * D_EXPERT ** -0.5),
        "ln2_g": 1.0 + nrm(ks[31], (L, D_MODEL), 0.01),
        "ln2_b": nrm(ks[32], (L, D_MODEL), 0.01),
    }


def reference(x, c, w_ada, b_ada, w_in, b_in, s5_a_re, s5_a_im, s5_log_dt, s5_b_re, s5_b_im,
              s5_c_re, s5_c_im, s5_d, w_s5_gate, w_s5_up, conv_dw, conv_dw_b, conv_ln_g,
              conv_ln_b, w_conv_out, w_out, ln1_g, ln1_b, w_route_group, b_route_group,
              w_route_expert, b_route_expert, w_exp_gate, w_exp_up, w_exp_down, ln2_g, ln2_b):
    c_act = jax.nn.silu(c)
    for l in range(DEPTH):
        shift1, scale1, gate1, shift2, scale2, gate2 = jnp.split(
            c_act @ w_ada[l] + b_ada[l], N_MOD, axis=-1)

        h = _modulate(x, shift1, scale1)
        proj = h @ w_in[l] + b_in[l]
        u_s5 = proj[..., :S5_WIDTH]
        z_conv = proj[..., S5_WIDTH:S5_WIDTH + 2 * CONV_CHANNELS]
        gate_logits = proj[..., S5_WIDTH + 2 * CONV_CHANNELS:]
        y_s5 = _s5_branch(u_s5, s5_a_re[l], s5_a_im[l], s5_log_dt[l], s5_b_re[l], s5_b_im[l],
                          s5_c_re[l], s5_c_im[l], s5_d[l], w_s5_gate[l]) @ w_s5_up[l]
        y_conv = _conv_branch(z_conv, conv_dw[l], conv_dw_b[l], conv_ln_g[l],
                              conv_ln_b[l]) @ w_conv_out[l]
        g_s5, g_conv = jnp.split(jax.nn.sigmoid(gate_logits), N_BRANCHES, axis=-1)
        mix = (g_s5 * y_s5 + g_conv * y_conv) @ w_out[l]
        x = _layernorm(DEEPNORM_ALPHA * x + (1 + gate1[:, None, :]) * mix, ln1_g[l], ln1_b[l])

        h = _modulate(x, shift2, scale2)
        ffn = _hier_moe(h, w_route_group[l], b_route_group[l], w_route_expert[l],
                        b_route_expert[l], w_exp_gate[l], w_exp_up[l], w_exp_down[l])
        x = _layernorm(DEEPNORM_ALPHA * x + (1 + gate2[:, None, :]) * ffn, ln2_g[l], ln2_b[l])
    return x
```

```python
import functools
import math

import jax
import jax.numpy as jnp
from jax import lax
from jax.experimental import pallas as pl
from jax.experimental.pallas import tpu as pltpu

F32 = jnp.float32
BF16 = jnp.bfloat16
LN_EPS = 1e-5
TOP_K = 2
LANES = 128
S5_CHUNK = 16
CONV_HALO = 32
MOE_ROWS = 256
VMEM_LIMIT = 56 * 1024 * 1024


def _cparams(*sem):
    return pltpu.CompilerParams(dimension_semantics=sem, vmem_limit_bytes=VMEM_LIMIT)


def _ln(x):
    mu = jnp.mean(x, axis=-1, keepdims=True)
    xc = x - mu
    var = jnp.mean(xc * xc, axis=-1, keepdims=True)
    return xc * lax.rsqrt(var + LN_EPS)


def _dot(a, b):
    return jnp.dot(a, b, preferred_element_type=F32)


def _ada_kernel(c_ref, w_ref, b_ref, o_ref):
    c = c_ref[...]
    o_ref[...] = _dot(c * jax.nn.sigmoid(c), w_ref[...]) + b_ref[...]


def _ada(c, w, b, tn=1024):
    bsz, d = c.shape
    n = w.shape[1]
    rows = 8
    cp = jnp.zeros((rows, d), F32).at[:bsz].set(c)
    out = pl.pallas_call(
        _ada_kernel,
        grid=(n // tn,),
        in_specs=[pl.BlockSpec((rows, d), lambda j: (0, 0)),
                  pl.BlockSpec((d, tn), lambda j: (0, j)),
                  pl.BlockSpec((1, tn), lambda j: (0, j))],
        out_specs=pl.BlockSpec((rows, tn), lambda j: (0, j)),
        out_shape=jax.ShapeDtypeStruct((rows, n), F32),
        compiler_params=_cparams("arbitrary"),
        name="ada",
    )(cp, w, b.reshape(1, n))
    return out[:bsz]


def _proj_kernel(x_ref, shift_ref, scale_ref, w_ref, b_ref, u_ref, v_ref, g_ref, h_scr, a_scr,
                 *, n_u, n_a):
    j = pl.program_id(1)

    @pl.when(j == 0)
    def _():
        h = _ln(x_ref[...]) * (1.0 + scale_ref[0]) + shift_ref[0]
        h_scr[...] = h.astype(BF16)

    p = _dot(h_scr[...], w_ref[...]) + b_ref[...]

    @pl.when(j < n_u)
    def _():
        u_ref[...] = p

    @pl.when((j >= n_u) & (j < n_u + n_a))
    def _():
        a_scr[j - n_u] = p

    @pl.when((j >= n_u + n_a) & (j < n_u + 2 * n_a))
    def _():
        v_ref[...] = a_scr[j - n_u - n_a] * jax.nn.sigmoid(p)

    @pl.when(j >= n_u + 2 * n_a)
    def _():
        g_ref[...] = jax.nn.sigmoid(p).astype(BF16)


def _proj(x2, shift, scale, w_bf, b, s5_width, conv_ch, seq, tm, tn):
    t, d = x2.shape
    n = w_bf.shape[1]
    n_u, n_a = s5_width // tn, conv_ch // tn
    n_g = (n - s5_width - 2 * conv_ch) // tn
    tiles_per_batch = seq // tm
    g0 = n_u + 2 * n_a
    kern = functools.partial(_proj_kernel, n_u=n_u, n_a=n_a)
    return pl.pallas_call(
        kern,
        grid=(t // tm, n // tn),
        in_specs=[pl.BlockSpec((tm, d), lambda i, j: (i, 0)),
                  pl.BlockSpec((1, 1, d), lambda i, j: (i // tiles_per_batch, 0, 0)),
                  pl.BlockSpec((1, 1, d), lambda i, j: (i // tiles_per_batch, 0, 0)),
                  pl.BlockSpec((d, tn), lambda i, j: (0, j)),
                  pl.BlockSpec((1, tn), lambda i, j: (0, j))],
        out_specs=[pl.BlockSpec((tm, tn), lambda i, j: (i, jnp.clip(j, 0, n_u - 1))),
                   pl.BlockSpec((tm, tn), lambda i, j: (i, jnp.clip(j - n_u - n_a, 0, n_a - 1))),
                   pl.BlockSpec((tm, tn), lambda i, j: (i, jnp.clip(j - g0, 0, n_g - 1)))],
        out_shape=[jax.ShapeDtypeStruct((t, s5_width), F32),
                   jax.ShapeDtypeStruct((t, conv_ch), F32),
                   jax.ShapeDtypeStruct((t, n_g * tn), BF16)],
        scratch_shapes=[pltpu.VMEM((tm, d), BF16), pltpu.VMEM((n_a, tm, tn), F32)],
        compiler_params=_cparams("arbitrary", "arbitrary"),
        name="proj",
    )(x2, shift, scale, w_bf, b.reshape(1, n))


def _s5_tables(a_re, a_im, log_dt, b_re, b_im, c_re, c_im, d_skip, chunk):
    hp = lax.Precision.HIGHEST
    g, p = a_re.shape
    hw = b_re.shape[-1]
    dt = jnp.exp(log_dt.astype(F32))[:, None]
    lr, li = a_re.astype(F32), a_im.astype(F32)
    k = jnp.arange(chunk + 1, dtype=F32)[:, None, None]
    mag = jnp.exp(k * (lr * dt)[None])
    ang = k * (li * dt)[None]
    pw_re, pw_im = mag * jnp.cos(ang), mag * jnp.sin(ang)
    den = lr * lr + li * li
    nr, ni = pw_re[1] - 1.0, pw_im[1]
    q_re = ((nr * lr + ni * li) / den)[:, :, None]
    q_im = ((ni * lr - nr * li) / den)[:, :, None]
    br, bi = b_re.astype(F32), b_im.astype(F32)
    bb_re = q_re * br - q_im * bi
    bb_im = q_re * bi + q_im * br
    cr, ci = c_re.astype(F32)[None], c_im.astype(F32)[None]
    ca_re = cr * pw_re[:, :, None, :] - ci * pw_im[:, :, None, :]
    ca_im = cr * pw_im[:, :, None, :] + ci * pw_re[:, :, None, :]
    kern = (jnp.einsum('kghp,gpj->kghj', ca_re[:chunk], bb_re, precision=hp)
            - jnp.einsum('kghp,gpj->kghj', ca_im[:chunk], bb_im, precision=hp))
    lag = jnp.arange(chunk)[None, :] - jnp.arange(chunk)[:, None]
    toe = kern[jnp.clip(lag, 0, chunk - 1)]
    toe = jnp.where((lag >= 0)[:, :, None, None, None], toe, 0.0)
    m = toe.transpose(2, 0, 4, 1, 3).reshape(g, chunk * hw, chunk * hw)
    rev_re, rev_im = pw_re[chunk - 1::-1][:chunk], pw_im[chunk - 1::-1][:chunk]
    ab_re = rev_re[..., None] * bb_re[None] - rev_im[..., None] * bb_im[None]
    ab_im = rev_re[..., None] * bb_im[None] + rev_im[..., None] * bb_re[None]
    p_re = ab_re.transpose(1, 0, 3, 2).reshape(g, chunk * hw, p)
    p_im = ab_im.transpose(1, 0, 3, 2).reshape(g, chunk * hw, p)
    qo_re = ca_re[1:].transpose(1, 3, 0, 2).reshape(g, p, chunk * hw)
    qo_im = -ca_im[1:].transpose(1, 3, 0, 2).reshape(g, p, chunk * hw)
    al_re, al_im = pw_re[chunk][:, None, :], pw_im[chunk][:, None, :]
    d_t = jnp.tile(d_skip.astype(F32)[:, None, :], (1, 1, chunk))
    return m, p_re, p_im, qo_re, qo_im, al_re, al_im, d_t


def _s5_kernel(u_ref, m_ref, pre_ref, pim_ref, qre_ref, qim_ref, are_ref, aim_ref, d_ref, y_ref,
               zre, zim, sre, sim, *, bsz):
    u = u_ref[0]
    rows = u.shape[0]
    nc = rows // bsz
    y = _dot(u, m_ref[0]) + d_ref[0] * u
    zre[...] = _dot(u, pre_ref[0])
    zim[...] = _dot(u, pim_ref[0])
    ar, ai = are_ref[0], aim_ref[0]
    npair = zre.shape[1]

    def step(c, carry):
        nxt = []
        for b in range(bsz):
            sr, si = carry[2 * b], carry[2 * b + 1]
            row = pl.ds(b * nc + c, 1)
            sre[row, :] = sr
            sim[row, :] = si
            nxt.append(ar * sr - ai * si + zre[row, :])
            nxt.append(ar * si + ai * sr + zim[row, :])
        return tuple(nxt)

    zero = jnp.zeros((1, npair), F32)
    lax.fori_loop(0, nc, step, tuple(zero for _ in range(2 * bsz)))
    y = y + _dot(sre[...], qre_ref[0]) + _dot(sim[...], qim_ref[0])
    y_ref[0] = jax.nn.gelu(y, approximate=True)


def _s5(u, tables, bsz, chunk):
    m, p_re, p_im, q_re, q_im, a_re, a_im, d_t = tables
    t, width = u.shape
    g, cw, _ = m.shape
    hw = cw // chunk
    npair = p_re.shape[-1]
    rows = t // chunk
    ut = u.reshape(rows, chunk, g, hw).transpose(2, 0, 1, 3).reshape(g, rows, cw)
    gspec = lambda *shape: pl.BlockSpec((1,) + shape, lambda i: (i, 0, 0))
    yt = pl.pallas_call(
        functools.partial(_s5_kernel, bsz=bsz),
        grid=(g,),
        in_specs=[gspec(rows, cw), gspec(cw, cw), gspec(cw, npair), gspec(cw, npair),
                  gspec(npair, cw), gspec(npair, cw), gspec(1, npair), gspec(1, npair), gspec(1, cw)],
        out_specs=gspec(rows, cw),
        out_shape=jax.ShapeDtypeStruct((g, rows, cw), F32),
        scratch_shapes=[pltpu.VMEM((rows, npair), F32) for _ in range(4)],
        compiler_params=_cparams("arbitrary"),
        name="s5",
    )(ut, m, p_re, p_im, q_re, q_im, a_re, a_im, d_t)
    return yt.reshape(g, rows, chunk, hw).transpose(1, 2, 0, 3).reshape(t, width)


def _conv_kernel(v_ref, w_ref, b_ref, g_ref, beta_ref, o_ref, win, *, span):
    ts = v_ref.shape[1]

    @pl.when(pl.program_id(1) == 0)
    def _():
        win[0:CONV_HALO, :] = jnp.zeros((CONV_HALO, win.shape[1]), F32)

    win[CONV_HALO:CONV_HALO + ts, :] = v_ref[0]
    base = CONV_HALO - (span - 1)
    acc = win[base:base + ts, :] * w_ref[0:1, :]
    for k in range(1, span):
        acc = acc + win[base + k:base + k + ts, :] * w_ref[k:k + 1, :]
    y = _ln(acc + b_ref[...]) * g_ref[...] + beta_ref[...]
    o_ref[0] = y * jax.nn.sigmoid(y)
    win[0:CONV_HALO, :] = win[ts:ts + CONV_HALO, :]


def _conv(v, w_dw, b_dw, ln_g, ln_b, ts):
    bsz, seq, ch = v.shape
    span = w_dw.shape[0]
    row = lambda a: a.reshape(1, ch)
    return pl.pallas_call(
        functools.partial(_conv_kernel, span=span),
        grid=(bsz, seq // ts),
        in_specs=[pl.BlockSpec((1, ts, ch), lambda b, s: (b, s, 0)),
                  pl.BlockSpec((span, ch), lambda b, s: (0, 0)),
                  pl.BlockSpec((1, ch), lambda b, s: (0, 0)),
                  pl.BlockSpec((1, ch), lambda b, s: (0, 0)),
                  pl.BlockSpec((1, ch), lambda b, s: (0, 0))],
        out_specs=pl.BlockSpec((1, ts, ch), lambda b, s: (b, s, 0)),
        out_shape=jax.ShapeDtypeStruct((bsz, seq, ch), F32),
        scratch_shapes=[pltpu.VMEM((CONV_HALO + ts, ch), F32)],
        compiler_params=_cparams("arbitrary", "arbitrary"),
        name="conv",
    )(v, w_dw.reshape(span, ch), row(b_dw), row(ln_g), row(ln_b))


def _branch_kernel(ys_ref, ca_ref, gs_ref, wgate_ref, wup_ref, wco_ref, o_ref):
    ys = ys_ref[...]
    glu = ys * jax.nn.sigmoid(_dot(ys.astype(BF16), wgate_ref[...]))
    y_s5 = _dot(glu.astype(BF16), wup_ref[...])
    y_conv = _dot(ca_ref[...].astype(BF16), wco_ref[...])
    d = y_s5.shape[1]
    o_ref[...] = gs_ref[:, :d].astype(F32) * y_s5 + gs_ref[:, d:].astype(F32) * y_conv


def _branch(ys, cact, gates, w_gate, w_up, w_co, tm):
    t, sw = ys.shape
    cc = cact.shape[1]
    d = w_up.shape[1]
    full = lambda a: pl.BlockSpec(a.shape, lambda i: (0, 0))
    return pl.pallas_call(
        _branch_kernel,
        grid=(t // tm,),
        in_specs=[pl.BlockSpec((tm, sw), lambda i: (i, 0)),
                  pl.BlockSpec((tm, cc), lambda i: (i, 0)),
                  pl.BlockSpec((tm, 2 * d), lambda i: (i, 0)),
                  full(w_gate), full(w_up), full(w_co)],
        out_specs=pl.BlockSpec((tm, d), lambda i: (i, 0)),
        out_shape=jax.ShapeDtypeStruct((t, d), F32),
        compiler_params=_cparams("arbitrary"),
        name="branch",
    )(ys, cact, gates, w_gate, w_up, w_co)


def _mix_kernel(m_ref, x_ref, gate_ref, g1_ref, b1_ref, scale2_ref, shift2_ref, wout_ref, wr_ref, br_ref,
                x1_ref, h2_ref, route_ref, cnt_ref, run, *, alpha, n_exp, n_grp):
    @pl.when(pl.program_id(0) == 0)
    def _():
        run[...] = jnp.zeros_like(run)

    mix = _dot(m_ref[...].astype(BF16), wout_ref[...])
    x1 = _ln(alpha * x_ref[...] + (1.0 + gate_ref[0]) * mix) * g1_ref[...] + b1_ref[...]
    x1_ref[...] = x1
    h2 = _ln(x1) * (1.0 + scale2_ref[0]) + shift2_ref[0]
    h2_ref[...] = h2

    logits = _dot(h2, wr_ref[...]) + br_ref[...]
    tm = logits.shape[0]
    per = n_exp // n_grp
    lane = lax.broadcasted_iota(jnp.int32, logits.shape, 1)
    big = jnp.int32(LANES)
    neg = jnp.float32(-jnp.inf)

    def first_max(val):
        top = jnp.max(val, axis=-1, keepdims=True)
        return top, jnp.min(jnp.where(val == top, lane, big), axis=-1, keepdims=True)

    gmask = (lane >= n_exp) & (lane < n_exp + n_grp)
    gtop, glane = first_max(jnp.where(gmask, logits, neg))
    grp = glane - n_exp
    grp_w = 1.0 / jnp.sum(jnp.where(gmask, jnp.exp(logits - gtop), 0.0), axis=-1, keepdims=True)
    emask = (lane >= grp * per) & (lane < grp * per + per)
    el = jnp.where(emask, logits, neg)
    t1, e1 = first_max(el)
    t2, e2 = first_max(jnp.where(lane == e1, neg, el))
    ex = jnp.exp(t2 - t1)
    w1 = grp_w / (1.0 + ex)
    w2 = grp_w * ex / (1.0 + ex)

    oh1 = (lane == e1).astype(F32)
    oh2 = (lane == e2).astype(F32)
    both = oh1 + oh2
    r_i = lax.broadcasted_iota(jnp.int32, (tm, tm), 0)
    c_i = lax.broadcasted_iota(jnp.int32, (tm, tm), 1)
    before = _dot((r_i > c_i).astype(F32), both) + run[...]
    rank1 = jnp.sum(before * oh1, axis=-1, keepdims=True)
    rank2 = jnp.sum(before * oh2, axis=-1, keepdims=True)
    run[...] = run[...] + jnp.sum(both, axis=0, keepdims=True)
    cnt_ref[...] = run[...]

    route = jnp.where(lane == 0, e1.astype(F32), 0.0)
    route = jnp.where(lane == 1, e2.astype(F32), route)
    route = jnp.where(lane == 2, w1, route)
    route = jnp.where(lane == 3, w2, route)
    route = jnp.where(lane == 4, rank1, route)
    route = jnp.where(lane == 5, rank2, route)
    route_ref[...] = route


def _mix(merged, x2, gate1, g1, b1, scale2, shift2, w_out, w_r, b_r, seq, tm, alpha, n_exp, n_grp):
    t, d = x2.shape
    tiles_per_batch = seq // tm
    tile = pl.BlockSpec((tm, d), lambda i: (i, 0))
    per_batch = pl.BlockSpec((1, 1, d), lambda i: (i // tiles_per_batch, 0, 0))
    const = lambda a: pl.BlockSpec(a.shape, lambda i: (0, 0))
    g1, b1 = g1.reshape(1, d), b1.reshape(1, d)
    kern = functools.partial(_mix_kernel, alpha=alpha, n_exp=n_exp, n_grp=n_grp)
    return pl.pallas_call(
        kern,
        grid=(t // tm,),
        in_specs=[tile, tile, per_batch, const(g1), const(b1), per_batch, per_batch,
                  const(w_out), const(w_r), const(b_r)],
        out_specs=[tile, tile, pl.BlockSpec((tm, LANES), lambda i: (i, 0)),
                   pl.BlockSpec((1, LANES), lambda i: (0, 0))],
        out_shape=[jax.ShapeDtypeStruct((t, d), F32), jax.ShapeDtypeStruct((t, d), F32),
                   jax.ShapeDtypeStruct((t, LANES), F32), jax.ShapeDtypeStruct((1, LANES), F32)],
        scratch_shapes=[pltpu.VMEM((1, LANES), F32)],
        compiler_params=_cparams("arbitrary"),
        name="mix_route",
    )(merged, x2, gate1, g1, b1, scale2, shift2, w_out, w_r, b_r)


def _moe_kernel(bexp_ref, nused_ref, tok_ref, slot_ref, h_hbm, wg_ref, wu_ref, wd_ref, y_hbm,
                xbuf, ybuf, gsem, ssem):
    del bexp_ref
    rows = xbuf.shape[0]

    @pl.when(pl.program_id(0) < nused_ref[0])
    def _():
        def gather(r, c):
            pltpu.make_async_copy(h_hbm.at[pl.ds(tok_ref[0, 0, r], 1)], xbuf.at[pl.ds(r, 1)], gsem).start()
            return c

        lax.fori_loop(0, rows, gather, 0)
        pltpu.make_async_copy(h_hbm.at[pl.ds(0, rows)], xbuf, gsem).wait()
        x = xbuf[...]
        hg = _dot(x, wg_ref[0])
        act = hg * jax.nn.sigmoid(hg) * _dot(x, wu_ref[0])
        ybuf[...] = _dot(act, wd_ref[0])

        def scatter(r, c):
            pltpu.make_async_copy(ybuf.at[pl.ds(r, 1)], y_hbm.at[pl.ds(slot_ref[0, 0, r], 1)], ssem).start()
            return c

        lax.fori_loop(0, rows, scatter, 0)
        pltpu.make_async_copy(ybuf, y_hbm.at[pl.ds(0, rows)], ssem).wait()


def _moe(h2, row_tok, row_slot, block_expert, n_used, w_g, w_u, w_d, n_slots):
    t, d = h2.shape
    n_blocks = row_tok.shape[0]
    de = w_g.shape[-1]
    smem = lambda: pl.BlockSpec((1, 1, MOE_ROWS), lambda b, e, n: (b, 0, 0), memory_space=pltpu.SMEM)
    grid_spec = pltpu.PrefetchScalarGridSpec(
        num_scalar_prefetch=2,
        grid=(n_blocks,),
        in_specs=[smem(), smem(),
                  pl.BlockSpec(memory_space=pl.ANY),
                  pl.BlockSpec((1, d, de), lambda b, e, n: (e[b], 0, 0)),
                  pl.BlockSpec((1, d, de), lambda b, e, n: (e[b], 0, 0)),
                  pl.BlockSpec((1, de, d), lambda b, e, n: (e[b], 0, 0))],
        out_specs=pl.BlockSpec(memory_space=pl.ANY),
        scratch_shapes=[pltpu.VMEM((MOE_ROWS, d), F32), pltpu.VMEM((MOE_ROWS, d), F32),
                        pltpu.SemaphoreType.DMA, pltpu.SemaphoreType.DMA],
    )
    return pl.pallas_call(
        _moe_kernel,
        grid_spec=grid_spec,
        out_shape=jax.ShapeDtypeStruct((n_slots + MOE_ROWS, d), F32),
        compiler_params=_cparams("arbitrary"),
        name="moe",
    )(block_expert, n_used, row_tok, row_slot, h2, w_g, w_u, w_d)


def _dispatch_tables(route, counts, n_exp, t):
    n_slots = t * TOP_K
    n_blocks = n_slots // MOE_ROWS + n_exp
    slot_e = route[:, 0:TOP_K].astype(jnp.int32)
    rank = route[:, 4:4 + TOP_K].astype(jnp.int32)
    cnt = counts[0, :n_exp].astype(jnp.int32)
    padded = (cnt + MOE_ROWS - 1) // MOE_ROWS * MOE_ROWS
    pends = jnp.cumsum(padded)
    dest = ((pends - padded)[slot_e] + rank).reshape(-1)
    spare = n_slots + jnp.arange(n_blocks * MOE_ROWS, dtype=jnp.int32) % MOE_ROWS
    row_slot = spare.at[dest].set(jnp.arange(n_slots, dtype=jnp.int32))
    row_tok = jnp.where(row_slot < n_slots, row_slot // TOP_K, 0)
    block_start = jnp.arange(n_blocks, dtype=jnp.int32) * MOE_ROWS
    block_expert = jnp.minimum(jnp.searchsorted(pends, block_start, side='right'), n_exp - 1).astype(jnp.int32)
    n_used = (pends[-1:] // MOE_ROWS).astype(jnp.int32)
    shape3 = (n_blocks, 1, MOE_ROWS)
    return row_tok.reshape(shape3), row_slot.reshape(shape3), block_expert, n_used


def _final_kernel(y_ref, route_ref, x1_ref, gate_ref, g_ref, b_ref, o_ref, *, alpha):
    d = x1_ref.shape[1]
    route = route_ref[...]
    ffn = route[:, 2:3] * y_ref[:, :d] + route[:, 3:4] * y_ref[:, d:]
    o_ref[...] = _ln(alpha * x1_ref[...] + (1.0 + gate_ref[0]) * ffn) * g_ref[...] + b_ref[...]


def _final(y_pairs, route, x1, gate2, g2, b2, seq, tm, alpha):
    t, d = x1.shape
    tiles_per_batch = seq // tm
    return pl.pallas_call(
        functools.partial(_final_kernel, alpha=alpha),
        grid=(t // tm,),
        in_specs=[pl.BlockSpec((tm, 2 * d), lambda i: (i, 0)),
                  pl.BlockSpec((tm, LANES), lambda i: (i, 0)),
                  pl.BlockSpec((tm, d), lambda i: (i, 0)),
                  pl.BlockSpec((1, 1, d), lambda i: (i // tiles_per_batch, 0, 0)),
                  pl.BlockSpec((1, d), lambda i: (0, 0)),
                  pl.BlockSpec((1, d), lambda i: (0, 0))],
        out_specs=pl.BlockSpec((tm, d), lambda i: (i, 0)),
        out_shape=jax.ShapeDtypeStruct((t, d), F32),
        compiler_params=_cparams("arbitrary"),
        name="final",
    )(y_pairs, route, x1, gate2, g2.reshape(1, d), b2.reshape(1, d))


def _tiles(seq):
    return dict(proj=min(1024, seq), conv=min(512, seq), branch=min(256, seq), mix=min(256, seq),
                final=min(512, seq))


def kernel(x, c, w_ada, b_ada, w_in, b_in, s5_a_re, s5_a_im, s5_log_dt, s5_b_re, s5_b_im, s5_c_re, s5_c_im, s5_d, w_s5_gate, w_s5_up, conv_dw, conv_dw_b, conv_ln_g, conv_ln_b, w_conv_out, w_out, ln1_g, ln1_b, w_route_group, b_route_group, w_route_expert, b_route_expert, w_exp_gate, w_exp_up, w_exp_down, ln2_g, ln2_b):
    bsz, seq, d = x.shape
    t = bsz * seq
    depth = w_ada.shape[0]
    alpha = (2.0 * depth) ** 0.25
    s5_width = w_s5_gate.shape[1]
    conv_ch = conv_dw.shape[-1]
    n_grp = w_route_group.shape[-1]
    n_exp = w_route_expert.shape[-1]
    tl = _tiles(seq)
    tn = min(512, s5_width)
    x2 = x.reshape(t, d)
    for l in range(depth):
        mod = _ada(c, w_ada[l], b_ada[l])
        shift1, scale1, gate1, shift2, scale2, gate2 = [
            m.reshape(bsz, 1, d) for m in jnp.split(mod, 6, axis=-1)]

        u, v, gates = _proj(x2, shift1, scale1, w_in[l].astype(BF16), b_in[l], s5_width, conv_ch, seq,
                            tl["proj"], tn)
        tables = _s5_tables(s5_a_re[l], s5_a_im[l], s5_log_dt[l], s5_b_re[l], s5_b_im[l],
                            s5_c_re[l], s5_c_im[l], s5_d[l], S5_CHUNK)
        ys = _s5(u, tables, bsz, S5_CHUNK)
        cact = _conv(v.reshape(bsz, seq, conv_ch), conv_dw[l], conv_dw_b[l], conv_ln_g[l], conv_ln_b[l],
                     tl["conv"]).reshape(t, conv_ch)
        merged = _branch(ys, cact, gates, w_s5_gate[l].astype(BF16), w_s5_up[l].astype(BF16),
                         w_conv_out[l].astype(BF16), tl["branch"])

        w_r = jnp.zeros((d, LANES), F32).at[:, :n_exp].set(w_route_expert[l])
        w_r = w_r.at[:, n_exp:n_exp + n_grp].set(w_route_group[l])
        b_r = jnp.zeros((1, LANES), F32).at[0, :n_exp].set(b_route_expert[l])
        b_r = b_r.at[0, n_exp:n_exp + n_grp].set(b_route_group[l])
        x1, h2, route, counts = _mix(merged, x2, gate1, ln1_g[l], ln1_b[l], scale2, shift2,
                                     w_out[l].astype(BF16), w_r, b_r, seq, tl["mix"], alpha, n_exp, n_grp)
        row_tok, row_slot, block_expert, n_used = _dispatch_tables(route, counts, n_exp, t)
        y_slots = _moe(h2, row_tok, row_slot, block_expert, n_used, w_exp_gate[l], w_exp_up[l],
                       w_exp_down[l], t * TOP_K)
        y_pairs = y_slots.reshape(-1, TOP_K * d)
        x2 = _final(y_pairs, route, x1, gate2, ln2_g[l], ln2_b[l], seq, tl["final"], alpha)
    return x2.reshape(bsz, seq, d)
```

```python
import functools
import math

import jax
import jax.numpy as jnp
from jax import lax
from jax.experimental import pallas as pl
from jax.experimental.pallas import tpu as pltpu

F32 = jnp.float32
BF16 = jnp.bfloat16
LN_EPS = 1e-5
TOP_K = 2
LANES = 128
S5_CHUNK = 16
CONV_HALO = 32
MOE_ROWS = 256
VMEM_LIMIT = 56 * 1024 * 1024


def _cparams(*sem):
    return pltpu.CompilerParams(dimension_semantics=sem, vmem_limit_bytes=VMEM_LIMIT)


def _ln(x):
    mu = jnp.mean(x, axis=-1, keepdims=True)
    xc = x - mu
    var = jnp.mean(xc * xc, axis=-1, keepdims=True)
    return xc * lax.rsqrt(var + LN_EPS)


def _dot(a, b):
    return jnp.dot(a, b, preferred_element_type=F32)


def _ada_kernel(c_ref, w_ref, b_ref, o_ref):
    c = c_ref[...]
    o_ref[...] = _dot(c * jax.nn.sigmoid(c), w_ref[...]) + b_ref[...]


def _ada(c, w, b, tn=1024):
    bsz, d = c.shape
    n = w.shape[1]
    rows = 8
    cp = jnp.zeros((rows, d), F32).at[:bsz].set(c)
    out = pl.pallas_call(
        _ada_kernel,
        grid=(n // tn,),
        in_specs=[pl.BlockSpec((rows, d), lambda j: (0, 0)),
                  pl.BlockSpec((d, tn), lambda j: (0, j)),
                  pl.BlockSpec((1, tn), lambda j: (0, j))],
        out_specs=pl.BlockSpec((rows, tn), lambda j: (0, j)),
        out_shape=jax.ShapeDtypeStruct((rows, n), F32),
        compiler_params=_cparams("arbitrary"),
        name="ada",
    )(cp, w, b.reshape(1, n))
    return out[:bsz]


def _proj_kernel(x_ref, shift_ref, scale_ref, w_ref, b_ref, u_ref, v_ref, g_ref, h_scr, a_scr,
                 *, n_u, n_a):
    j = pl.program_id(1)

    @pl.when(j == 0)
    def _():
        h = _ln(x_ref[...]) * (1.0 + scale_ref[0]) + shift_ref[0]
        h_scr[...] = h.astype(BF16)

    p = _dot(h_scr[...], w_ref[...]) + b_ref[...]

    @pl.when(j < n_u)
    def _():
        for q in range(u_ref.shape[0]):
            u_ref[q] = p[:, q * LANES:(q + 1) * LANES]

    @pl.when((j >= n_u) & (j < n_u + n_a))
    def _():
        a_scr[j - n_u] = p

    @pl.when((j >= n_u + n_a) & (j < n_u + 2 * n_a))
    def _():
        v_ref[...] = a_scr[j - n_u - n_a] * jax.nn.sigmoid(p)

    @pl.when(j >= n_u + 2 * n_a)
    def _():
        g_ref[...] = jax.nn.sigmoid(p).astype(BF16)


def _proj(x2, shift, scale, w_bf, b, s5_width, conv_ch, seq, tm, tn):
    t, d = x2.shape
    n = w_bf.shape[1]
    n_u, n_a = s5_width // tn, conv_ch // tn
    assert n_u == 1, "the S5 input must be one column tile"
    n_lt = s5_width // LANES
    n_g = (n - s5_width - 2 * conv_ch) // tn
    tiles_per_batch = seq // tm
    g0 = n_u + 2 * n_a
    kern = functools.partial(_proj_kernel, n_u=n_u, n_a=n_a)
    return pl.pallas_call(
        kern,
        grid=(t // tm, n // tn),
        in_specs=[pl.BlockSpec((tm, d), lambda i, j: (i, 0)),
                  pl.BlockSpec((1, 1, d), lambda i, j: (i // tiles_per_batch, 0, 0)),
                  pl.BlockSpec((1, 1, d), lambda i, j: (i // tiles_per_batch, 0, 0)),
                  pl.BlockSpec((d, tn), lambda i, j: (0, j)),
                  pl.BlockSpec((1, tn), lambda i, j: (0, j))],
        out_specs=[pl.BlockSpec((n_lt, tm, LANES), lambda i, j: (0, i, 0)),
                   pl.BlockSpec((tm, tn), lambda i, j: (i, jnp.clip(j - n_u - n_a, 0, n_a - 1))),
                   pl.BlockSpec((tm, tn), lambda i, j: (i, jnp.clip(j - g0, 0, n_g - 1)))],
        out_shape=[jax.ShapeDtypeStruct((n_lt, t, LANES), F32),
                   jax.ShapeDtypeStruct((t, conv_ch), F32),
                   jax.ShapeDtypeStruct((t, n_g * tn), BF16)],
        scratch_shapes=[pltpu.VMEM((tm, d), BF16), pltpu.VMEM((n_a, tm, tn), F32)],
        compiler_params=_cparams("arbitrary", "arbitrary"),
        name="proj",
    )(x2, shift, scale, w_bf, b.reshape(1, n))


def _s5_tables(a_re, a_im, log_dt, b_re, b_im, c_re, c_im, d_skip, chunk):
    hp = lax.Precision.HIGHEST
    g, p = a_re.shape
    hw = b_re.shape[-1]
    dt = jnp.exp(log_dt.astype(F32))[:, None]
    lr, li = a_re.astype(F32), a_im.astype(F32)
    k = jnp.arange(chunk + 1, dtype=F32)[:, None, None]
    mag = jnp.exp(k * (lr * dt)[None])
    ang = k * (li * dt)[None]
    pw_re, pw_im = mag * jnp.cos(ang), mag * jnp.sin(ang)
    den = lr * lr + li * li
    nr, ni = pw_re[1] - 1.0, pw_im[1]
    q_re = ((nr * lr + ni * li) / den)[:, :, None]
    q_im = ((ni * lr - nr * li) / den)[:, :, None]
    br, bi = b_re.astype(F32), b_im.astype(F32)
    bb_re = q_re * br - q_im * bi
    bb_im = q_re * bi + q_im * br
    cr, ci = c_re.astype(F32)[None], c_im.astype(F32)[None]
    ca_re = cr * pw_re[:, :, None, :] - ci * pw_im[:, :, None, :]
    ca_im = cr * pw_im[:, :, None, :] + ci * pw_re[:, :, None, :]
    kern = (jnp.einsum('kghp,gpj->kghj', ca_re[:chunk], bb_re, precision=hp)
            - jnp.einsum('kghp,gpj->kghj', ca_im[:chunk], bb_im, precision=hp))
    lag = jnp.arange(chunk)[None, :] - jnp.arange(chunk)[:, None]
    toe = kern[jnp.clip(lag, 0, chunk - 1)]
    toe = jnp.where((lag >= 0)[:, :, None, None, None], toe, 0.0)
    m = toe.transpose(2, 0, 4, 1, 3).reshape(g, chunk * hw, chunk * hw)
    rev_re, rev_im = pw_re[chunk - 1::-1][:chunk], pw_im[chunk - 1::-1][:chunk]
    ab_re = rev_re[..., None] * bb_re[None] - rev_im[..., None] * bb_im[None]
    ab_im = rev_re[..., None] * bb_im[None] + rev_im[..., None] * bb_re[None]
    p_re = ab_re.transpose(1, 0, 3, 2).reshape(g, chunk * hw, p)
    p_im = ab_im.transpose(1, 0, 3, 2).reshape(g, chunk * hw, p)
    qo_re = ca_re[1:].transpose(1, 3, 0, 2).reshape(g, p, chunk * hw)
    qo_im = -ca_im[1:].transpose(1, 3, 0, 2).reshape(g, p, chunk * hw)
    al_re, al_im = pw_re[chunk][:, None, :], pw_im[chunk][:, None, :]
    d_t = jnp.tile(d_skip.astype(F32)[:, None, :], (1, 1, chunk))
    return m, p_re, p_im, qo_re, qo_im, al_re, al_im, d_t


def _s5_kernel(u_ref, m_ref, pre_ref, pim_ref, qre_ref, qim_ref, are_ref, aim_ref, d_ref, y_ref,
               zre, zim, sre, sim, *, bsz):
    u = u_ref[0]
    rows = u.shape[0]
    nc = rows // bsz
    y = _dot(u, m_ref[0]) + d_ref[0] * u
    zre[...] = _dot(u, pre_ref[0])
    zim[...] = _dot(u, pim_ref[0])
    ar, ai = are_ref[0], aim_ref[0]
    npair = zre.shape[1]

    def step(c, carry):
        nxt = []
        for b in range(bsz):
            sr, si = carry[2 * b], carry[2 * b + 1]
            row = pl.ds(b * nc + c, 1)
            sre[row, :] = sr
            sim[row, :] = si
            nxt.append(ar * sr - ai * si + zre[row, :])
            nxt.append(ar * si + ai * sr + zim[row, :])
        return tuple(nxt)

    zero = jnp.zeros((1, npair), F32)
    lax.fori_loop(0, nc, step, tuple(zero for _ in range(2 * bsz)))
    y = y + _dot(sre[...], qre_ref[0]) + _dot(sim[...], qim_ref[0])
    y_ref[0] = jax.nn.gelu(y, approximate=True)


def _pack_kernel(u_ref, o_ref, *, chunk):
    g, cb, _ = o_ref.shape
    n_lt = u_ref.shape[0]
    per = g // n_lt
    hw = LANES // per
    for s in range(chunk):
        for q in range(n_lt):
            rows = u_ref[q, pl.ds(s, cb, stride=chunk), :]
            for i in range(per):
                o_ref[q * per + i, :, s * hw:(s + 1) * hw] = rows[:, i * hw:(i + 1) * hw]


def _unpack_kernel(y_ref, o_ref, tmp, *, chunk):
    g, cb, _ = y_ref.shape
    n_lt = o_ref.shape[0]
    per = g // n_lt
    hw = LANES // per
    for s in range(chunk):
        for q in range(n_lt):
            for i in range(per):
                tmp[:, i * hw:(i + 1) * hw] = y_ref[q * per + i, :, s * hw:(s + 1) * hw]
            o_ref[q, pl.ds(s, cb, stride=chunk), :] = tmp[...]


def _s5(u, tables, bsz, chunk):
    m, p_re, p_im, q_re, q_im, a_re, a_im, d_t = tables
    n_lt, t, _ = u.shape
    g, cw, _ = m.shape
    npair = p_re.shape[-1]
    rows = t // chunk
    cb = min(64, rows)
    by_chunk = pl.BlockSpec((n_lt, cb * chunk, LANES), lambda i: (0, i, 0))
    by_group = pl.BlockSpec((g, cb, cw), lambda i: (0, i, 0))
    ut = pl.pallas_call(
        functools.partial(_pack_kernel, chunk=chunk), grid=(rows // cb,), in_specs=[by_chunk],
        out_specs=by_group, out_shape=jax.ShapeDtypeStruct((g, rows, cw), F32),
        compiler_params=_cparams("arbitrary"), name="s5_pack",
    )(u)
    gspec = lambda *shape: pl.BlockSpec((1,) + shape, lambda i: (i, 0, 0))
    yt = pl.pallas_call(
        functools.partial(_s5_kernel, bsz=bsz),
        grid=(g,),
        in_specs=[gspec(rows, cw), gspec(cw, cw), gspec(cw, npair), gspec(cw, npair),
                  gspec(npair, cw), gspec(npair, cw), gspec(1, npair), gspec(1, npair), gspec(1, cw)],
        out_specs=gspec(rows, cw),
        out_shape=jax.ShapeDtypeStruct((g, rows, cw), F32),
        scratch_shapes=[pltpu.VMEM((rows, npair), F32) for _ in range(4)],
        compiler_params=_cparams("arbitrary"),
        name="s5",
    )(ut, m, p_re, p_im, q_re, q_im, a_re, a_im, d_t)
    return pl.pallas_call(
        functools.partial(_unpack_kernel, chunk=chunk), grid=(rows // cb,), in_specs=[by_group],
        out_specs=by_chunk, out_shape=jax.ShapeDtypeStruct((n_lt, t, LANES), F32),
        scratch_shapes=[pltpu.VMEM((cb, LANES), F32)],
        compiler_params=_cparams("arbitrary"), name="s5_unpack",
    )(yt)


def _conv_kernel(v_ref, w_ref, b_ref, g_ref, beta_ref, o_ref, win, *, span):
    ts = v_ref.shape[1]
    sub = 8

    @pl.when(pl.program_id(1) == 0)
    def _():
        win[0:CONV_HALO, :] = jnp.zeros((CONV_HALO, win.shape[1]), F32)
        win[CONV_HALO + ts:CONV_HALO + ts + sub, :] = jnp.zeros((sub, win.shape[1]), F32)

    win[CONV_HALO:CONV_HALO + ts, :] = v_ref[0]
    base = CONV_HALO - (span - 1)
    acc = None
    for q in range(sub):
        part = None
        for k in range(span):
            if (base + k) % sub == q:
                lo = base + k - q
                term = win[lo:lo + ts + sub, :] * w_ref[k:k + 1, :]
                part = term if part is None else part + term
        if part is not None:
            part = part[q:q + ts, :]
            acc = part if acc is None else acc + part
    y = _ln(acc + b_ref[...]) * g_ref[...] + beta_ref[...]
    o_ref[0] = y * jax.nn.sigmoid(y)
    win[0:CONV_HALO, :] = win[ts:ts + CONV_HALO, :]


def _conv(v, w_dw, b_dw, ln_g, ln_b, ts):
    bsz, seq, ch = v.shape
    span = w_dw.shape[0]
    row = lambda a: a.reshape(1, ch)
    return pl.pallas_call(
        functools.partial(_conv_kernel, span=span),
        grid=(bsz, seq // ts),
        in_specs=[pl.BlockSpec((1, ts, ch), lambda b, s: (b, s, 0)),
                  pl.BlockSpec((span, ch), lambda b, s: (0, 0)),
                  pl.BlockSpec((1, ch), lambda b, s: (0, 0)),
                  pl.BlockSpec((1, ch), lambda b, s: (0, 0)),
                  pl.BlockSpec((1, ch), lambda b, s: (0, 0))],
        out_specs=pl.BlockSpec((1, ts, ch), lambda b, s: (b, s, 0)),
        out_shape=jax.ShapeDtypeStruct((bsz, seq, ch), F32),
        scratch_shapes=[pltpu.VMEM((CONV_HALO + ts + 8, ch), F32)],
        compiler_params=_cparams("arbitrary", "arbitrary"),
        name="conv",
    )(v, w_dw.reshape(span, ch), row(b_dw), row(ln_g), row(ln_b))


def _branch_kernel(ys_ref, ca_ref, gs_ref, wgate_ref, wup_ref, wco_ref, o_ref):
    ys = jnp.concatenate([ys_ref[q] for q in range(ys_ref.shape[0])], axis=-1)
    glu = ys * jax.nn.sigmoid(_dot(ys.astype(BF16), wgate_ref[...]))
    y_s5 = _dot(glu.astype(BF16), wup_ref[...])
    y_conv = _dot(ca_ref[...].astype(BF16), wco_ref[...])
    d = y_s5.shape[1]
    o_ref[...] = gs_ref[:, :d].astype(F32) * y_s5 + gs_ref[:, d:].astype(F32) * y_conv


def _branch(ys, cact, gates, w_gate, w_up, w_co, tm):
    n_lt, t, _ = ys.shape
    cc = cact.shape[1]
    d = w_up.shape[1]
    full = lambda a: pl.BlockSpec(a.shape, lambda i: (0, 0))
    return pl.pallas_call(
        _branch_kernel,
        grid=(t // tm,),
        in_specs=[pl.BlockSpec((n_lt, tm, LANES), lambda i: (0, i, 0)),
                  pl.BlockSpec((tm, cc), lambda i: (i, 0)),
                  pl.BlockSpec((tm, 2 * d), lambda i: (i, 0)),
                  full(w_gate), full(w_up), full(w_co)],
        out_specs=pl.BlockSpec((tm, d), lambda i: (i, 0)),
        out_shape=jax.ShapeDtypeStruct((t, d), F32),
        compiler_params=_cparams("arbitrary"),
        name="branch",
    )(ys, cact, gates, w_gate, w_up, w_co)


def _mix_kernel(m_ref, x_ref, gate_ref, g1_ref, b1_ref, scale2_ref, shift2_ref, wout_ref, wr_ref, br_ref,
                x1_ref, h2_ref, route_ref, cnt_ref, run, *, alpha, n_exp, n_grp):
    @pl.when(pl.program_id(0) == 0)
    def _():
        run[...] = jnp.zeros_like(run)

    mix = _dot(m_ref[...].astype(BF16), wout_ref[...])
    x1 = _ln(alpha * x_ref[...] + (1.0 + gate_ref[0]) * mix) * g1_ref[...] + b1_ref[...]
    x1_ref[...] = x1
    h2 = _ln(x1) * (1.0 + scale2_ref[0]) + shift2_ref[0]
    h2_ref[...] = h2

    logits = _dot(h2, wr_ref[...]) + br_ref[...]
    tm = logits.shape[0]
    per = n_exp // n_grp
    lane = lax.broadcasted_iota(jnp.int32, logits.shape, 1)
    big = jnp.int32(LANES)
    neg = jnp.float32(-jnp.inf)

    def first_max(val):
        top = jnp.max(val, axis=-1, keepdims=True)
        return top, jnp.min(jnp.where(val == top, lane, big), axis=-1, keepdims=True)

    gmask = (lane >= n_exp) & (lane < n_exp + n_grp)
    gtop, glane = first_max(jnp.where(gmask, logits, neg))
    grp = glane - n_exp
    grp_w = 1.0 / jnp.sum(jnp.where(gmask, jnp.exp(logits - gtop), 0.0), axis=-1, keepdims=True)
    emask = (lane >= grp * per) & (lane < grp * per + per)
    el = jnp.where(emask, logits, neg)
    t1, e1 = first_max(el)
    t2, e2 = first_max(jnp.where(lane == e1, neg, el))
    ex = jnp.exp(t2 - t1)
    w1 = grp_w / (1.0 + ex)
    w2 = grp_w * ex / (1.0 + ex)

    oh1 = (lane == e1).astype(F32)
    oh2 = (lane == e2).astype(F32)
    both = oh1 + oh2
    r_i = lax.broadcasted_iota(jnp.int32, (tm, tm), 0)
    c_i = lax.broadcasted_iota(jnp.int32, (tm, tm), 1)
    before = _dot((r_i > c_i).astype(F32), both) + run[...]
    rank1 = jnp.sum(before * oh1, axis=-1, keepdims=True)
    rank2 = jnp.sum(before * oh2, axis=-1, keepdims=True)
    run[...] = run[...] + jnp.sum(both, axis=0, keepdims=True)
    cnt_ref[...] = run[...]

    route = jnp.where(lane == 0, e1.astype(F32), 0.0)
    route = jnp.where(lane == 1, e2.astype(F32), route)
    route = jnp.where(lane == 2, w1, route)
    route = jnp.where(lane == 3, w2, route)
    route = jnp.where(lane == 4, rank1, route)
    route = jnp.where(lane == 5, rank2, route)
    route_ref[...] = route


def _mix(merged, x2, gate1, g1, b1, scale2, shift2, w_out, w_r, b_r, seq, tm, alpha, n_exp, n_grp):
    t, d = x2.shape
    tiles_per_batch = seq // tm
    tile = pl.BlockSpec((tm, d), lambda i: (i, 0))
    per_batch = pl.BlockSpec((1, 1, d), lambda i: (i // tiles_per_batch, 0, 0))
    const = lambda a: pl.BlockSpec(a.shape, lambda i: (0, 0))
    g1, b1 = g1.reshape(1, d), b1.reshape(1, d)
    kern = functools.partial(_mix_kernel, alpha=alpha, n_exp=n_exp, n_grp=n_grp)
    return pl.pallas_call(
        kern,
        grid=(t // tm,),
        in_specs=[tile, tile, per_batch, const(g1), const(b1), per_batch, per_batch,
                  const(w_out), const(w_r), const(b_r)],
        out_specs=[tile, tile, pl.BlockSpec((tm, LANES), lambda i: (i, 0)),
                   pl.BlockSpec((1, LANES), lambda i: (0, 0))],
        out_shape=[jax.ShapeDtypeStruct((t, d), F32), jax.ShapeDtypeStruct((t, d), F32),
                   jax.ShapeDtypeStruct((t, LANES), F32), jax.ShapeDtypeStruct((1, LANES), F32)],
        scratch_shapes=[pltpu.VMEM((1, LANES), F32)],
        compiler_params=_cparams("arbitrary"),
        name="mix_route",
    )(merged, x2, gate1, g1, b1, scale2, shift2, w_out, w_r, b_r)


def _moe_kernel(bexp_ref, nused_ref, tok_ref, tok_next_ref, slot_ref, h_hbm, wg_ref, wu_ref, wd_ref, y_hbm,
                xbuf, ybuf, gsem, ssem):
    del bexp_ref
    rows = xbuf.shape[1]
    b = pl.program_id(0)
    n_used = nused_ref[0]
    cur, nxt = b % 2, (b + 1) % 2

    def gather_start(idx_ref, buf):
        for r in range(rows):
            pltpu.make_async_copy(h_hbm.at[pl.ds(idx_ref[0, 0, r], 1)], xbuf.at[buf, pl.ds(r, 1)],
                                  gsem.at[buf]).start()

    def gather_wait(buf):
        pltpu.make_async_copy(h_hbm.at[pl.ds(0, rows)], xbuf.at[buf], gsem.at[buf]).wait()

    def scatter_wait(buf):
        pltpu.make_async_copy(ybuf.at[buf], y_hbm.at[pl.ds(0, rows)], ssem.at[buf]).wait()

    @pl.when(b == 0)
    def _():
        ybuf[1] = jnp.zeros(ybuf.shape[1:], F32)
        spare = pltpu.make_async_copy(ybuf.at[1], y_hbm.at[pl.ds(y_hbm.shape[0] - rows, rows)], ssem.at[1])
        spare.start()
        spare.wait()
        gather_start(tok_ref, 0)

    @pl.when((b >= 2) & (b < n_used))
    def _():
        scatter_wait(cur)

    @pl.when(b < n_used)
    def _():
        gather_start(tok_next_ref, nxt)
        gather_wait(cur)
        x = xbuf[cur]
        hg = _dot(x, wg_ref[0])
        act = hg * jax.nn.sigmoid(hg) * _dot(x, wu_ref[0])
        ybuf[cur] = _dot(act, wd_ref[0])
        for r in range(rows):
            pltpu.make_async_copy(ybuf.at[cur, pl.ds(r, 1)], y_hbm.at[pl.ds(slot_ref[0, 0, r], 1)],
                                  ssem.at[cur]).start()

    @pl.when(b == n_used - 1)
    def _():
        gather_wait(nxt)
        scatter_wait(cur)

        @pl.when(b >= 1)
        def _():
            scatter_wait(nxt)


def _moe(h2, row_tok, row_slot, block_expert, n_used, w_g, w_u, w_d, n_slots):
    t, d = h2.shape
    n_blocks = row_tok.shape[0]
    de = w_g.shape[-1]
    smem = lambda off: pl.BlockSpec((1, 1, MOE_ROWS), lambda b, e, n: (jnp.minimum(b + off, n_blocks - 1), 0, 0),
                                    memory_space=pltpu.SMEM)
    grid_spec = pltpu.PrefetchScalarGridSpec(
        num_scalar_prefetch=2,
        grid=(n_blocks,),
        in_specs=[smem(0), smem(1), smem(0),
                  pl.BlockSpec(memory_space=pl.ANY),
                  pl.BlockSpec((1, d, de), lambda b, e, n: (e[b], 0, 0)),
                  pl.BlockSpec((1, d, de), lambda b, e, n: (e[b], 0, 0)),
                  pl.BlockSpec((1, de, d), lambda b, e, n: (e[b], 0, 0))],
        out_specs=pl.BlockSpec(memory_space=pl.ANY),
        scratch_shapes=[pltpu.VMEM((2, MOE_ROWS, d), F32), pltpu.VMEM((2, MOE_ROWS, d), F32),
                        pltpu.SemaphoreType.DMA((2,)), pltpu.SemaphoreType.DMA((2,))],
    )
    return pl.pallas_call(
        _moe_kernel,
        grid_spec=grid_spec,
        out_shape=jax.ShapeDtypeStruct((n_slots + MOE_ROWS, d), F32),
        compiler_params=_cparams("arbitrary"),
        name="moe",
    )(block_expert, n_used, row_tok, row_tok, row_slot, h2, w_g, w_u, w_d)


def _dispatch_tables(route, counts, n_exp, t):
    n_slots = t * TOP_K
    n_blocks = n_slots // MOE_ROWS + n_exp
    slot_e = route[:, 0:TOP_K].astype(jnp.int32)
    rank = route[:, 4:4 + TOP_K].astype(jnp.int32)
    cnt = counts[0, :n_exp].astype(jnp.int32)
    padded = (cnt + MOE_ROWS - 1) // MOE_ROWS * MOE_ROWS
    pends = jnp.cumsum(padded)
    dest = ((pends - padded)[slot_e] + rank).reshape(-1)
    spare = n_slots + jnp.arange(n_blocks * MOE_ROWS, dtype=jnp.int32) % MOE_ROWS
    row_slot = spare.at[dest].set(jnp.arange(n_slots, dtype=jnp.int32))
    row_tok = jnp.where(row_slot < n_slots, row_slot // TOP_K, 0)
    block_start = jnp.arange(n_blocks, dtype=jnp.int32) * MOE_ROWS
    block_expert = jnp.minimum(jnp.searchsorted(pends, block_start, side='right'), n_exp - 1).astype(jnp.int32)
    n_used = (pends[-1:] // MOE_ROWS).astype(jnp.int32)
    shape3 = (n_blocks, 1, MOE_ROWS)
    return row_tok.reshape(shape3), row_slot.reshape(shape3), block_expert, n_used


def _final_kernel(y_ref, route_ref, x1_ref, gate_ref, g_ref, b_ref, o_ref, *, alpha):
    d = x1_ref.shape[1]
    route = route_ref[...]
    ffn = route[:, 2:3] * y_ref[:, :d] + route[:, 3:4] * y_ref[:, d:]
    o_ref[...] = _ln(alpha * x1_ref[...] + (1.0 + gate_ref[0]) * ffn) * g_ref[...] + b_ref[...]


def _final(y_pairs, route, x1, gate2, g2, b2, seq, tm, alpha):
    t, d = x1.shape
    tiles_per_batch = seq // tm
    return pl.pallas_call(
        functools.partial(_final_kernel, alpha=alpha),
        grid=(t // tm,),
        in_specs=[pl.BlockSpec((tm, 2 * d), lambda i: (i, 0)),
                  pl.BlockSpec((tm, LANES), lambda i: (i, 0)),
                  pl.BlockSpec((tm, d), lambda i: (i, 0)),
                  pl.BlockSpec((1, 1, d), lambda i: (i // tiles_per_batch, 0, 0)),
                  pl.BlockSpec((1, d), lambda i: (0, 0)),
                  pl.BlockSpec((1, d), lambda i: (0, 0))],
        out_specs=pl.BlockSpec((tm, d), lambda i: (i, 0)),
        out_shape=jax.ShapeDtypeStruct((t, d), F32),
        compiler_params=_cparams("arbitrary"),
        name="final",
    )(y_pairs, route, x1, gate2, g2.reshape(1, d), b2.reshape(1, d))


def _tiles(seq):
    return dict(proj=min(1024, seq), conv=min(512, seq), branch=min(256, seq), mix=min(256, seq),
                final=min(512, seq))


def kernel(x, c, w_ada, b_ada, w_in, b_in, s5_a_re, s5_a_im, s5_log_dt, s5_b_re, s5_b_im, s5_c_re, s5_c_im, s5_d, w_s5_gate, w_s5_up, conv_dw, conv_dw_b, conv_ln_g, conv_ln_b, w_conv_out, w_out, ln1_g, ln1_b, w_route_group, b_route_group, w_route_expert, b_route_expert, w_exp_gate, w_exp_up, w_exp_down, ln2_g, ln2_b):
    bsz, seq, d = x.shape
    t = bsz * seq
    depth = w_ada.shape[0]
    alpha = (2.0 * depth) ** 0.25
    s5_width = w_s5_gate.shape[1]
    conv_ch = conv_dw.shape[-1]
    n_grp = w_route_group.shape[-1]
    n_exp = w_route_expert.shape[-1]
    tl = _tiles(seq)
    tn = min(512, s5_width)
    x2 = x.reshape(t, d)
    for l in range(depth):
        mod = _ada(c, w_ada[l], b_ada[l])
        shift1, scale1, gate1, shift2, scale2, gate2 = [
            m.reshape(bsz, 1, d) for m in jnp.split(mod, 6, axis=-1)]

        u, v, gates = _proj(x2, shift1, scale1, w_in[l].astype(BF16), b_in[l], s5_width, conv_ch, seq,
                            tl["proj"], tn)
        tables = _s5_tables(s5_a_re[l], s5_a_im[l], s5_log_dt[l], s5_b_re[l], s5_b_im[l],
                            s5_c_re[l], s5_c_im[l], s5_d[l], S5_CHUNK)
        ys = _s5(u, tables, bsz, S5_CHUNK)
        cact = _conv(v.reshape(bsz, seq, conv_ch), conv_dw[l], conv_dw_b[l], conv_ln_g[l], conv_ln_b[l],
                     tl["conv"]).reshape(t, conv_ch)
        merged = _branch(ys, cact, gates, w_s5_gate[l].astype(BF16), w_s5_up[l].astype(BF16),
                         w_conv_out[l].astype(BF16), tl["branch"])

        w_r = jnp.zeros((d, LANES), F32).at[:, :n_exp].set(w_route_expert[l])
        w_r = w_r.at[:, n_exp:n_exp + n_grp].set(w_route_group[l])
        b_r = jnp.zeros((1, LANES), F32).at[0, :n_exp].set(b_route_expert[l])
        b_r = b_r.at[0, n_exp:n_exp + n_grp].set(b_route_group[l])
        x1, h2, route, counts = _mix(merged, x2, gate1, ln1_g[l], ln1_b[l], scale2, shift2,
                                     w_out[l].astype(BF16), w_r, b_r, seq, tl["mix"], alpha, n_exp, n_grp)
        row_tok, row_slot, block_expert, n_used = _dispatch_tables(route, counts, n_exp, t)
        y_slots = _moe(h2, row_tok, row_slot, block_expert, n_used, w_exp_gate[l], w_exp_up[l],
                       w_exp_down[l], t * TOP_K)
        y_pairs = y_slots.reshape(-1, TOP_K * d)
        x2 = _final(y_pairs, route, x1, gate2, ln2_g[l], ln2_b[l], seq, tl["final"], alpha)
    return x2.reshape(bsz, seq, d)
```

```python
import functools
import math

import jax
import jax.numpy as jnp
from jax import lax
from jax.experimental import pallas as pl
from jax.experimental.pallas import tpu as pltpu

F32 = jnp.float32
BF16 = jnp.bfloat16
LN_EPS = 1e-5
TOP_K = 2
LANES = 128
S5_CHUNK = 16
CONV_HALO = 32
MOE_ROWS = 256
VMEM_LIMIT = 56 * 1024 * 1024


def _cparams(*sem):
    return pltpu.CompilerParams(dimension_semantics=sem, vmem_limit_bytes=VMEM_LIMIT)


def _ln(x):
    mu = jnp.mean(x, axis=-1, keepdims=True)
    xc = x - mu
    var = jnp.mean(xc * xc, axis=-1, keepdims=True)
    return xc * lax.rsqrt(var + LN_EPS)


def _dot(a, b):
    return jnp.dot(a, b, preferred_element_type=F32)


def _pack_pairs(a):
    w = a.shape[1] // 2
    hi = lax.bitcast_convert_type(a[:, :w].astype(BF16).astype(F32), jnp.uint32)
    lo = lax.bitcast_convert_type(a[:, w:].astype(BF16).astype(F32), jnp.uint32)
    return hi | (lo >> 16)


def _unpack_pairs(words):
    hi = lax.bitcast_convert_type(words & jnp.uint32(0xFFFF0000), F32)
    lo = lax.bitcast_convert_type(words << 16, F32)
    return jnp.concatenate([hi, lo], axis=-1)


def _store_rows(ref, first, n, k, packed):
    for j in range(k):
        ref[pl.ds(first + j, n, stride=k), :] = packed[:, j * LANES:(j + 1) * LANES]


def _load_rows(ref, first, n, k, stride):
    return jnp.concatenate([ref[pl.ds(first + j, n, stride=stride), :] for j in range(k)], axis=-1)


def _ada_kernel(c_ref, w_ref, b_ref, o_ref):
    c = c_ref[...]
    o_ref[...] = _dot(c * jax.nn.sigmoid(c), w_ref[...]) + b_ref[...]


def _ada(c, w, b, tn=1024):
    bsz, d = c.shape
    n = w.shape[1]
    rows = 8
    cp = jnp.zeros((rows, d), F32).at[:bsz].set(c)
    out = pl.pallas_call(
        _ada_kernel,
        grid=(n // tn,),
        in_specs=[pl.BlockSpec((rows, d), lambda j: (0, 0)),
                  pl.BlockSpec((d, tn), lambda j: (0, j)),
                  pl.BlockSpec((1, tn), lambda j: (0, j))],
        out_specs=pl.BlockSpec((rows, tn), lambda j: (0, j)),
        out_shape=jax.ShapeDtypeStruct((rows, n), F32),
        compiler_params=_cparams("arbitrary"),
        name="ada",
    )(cp, w, b.reshape(1, n))
    return out[:bsz]


def _proj_kernel(x_ref, shift_ref, scale_ref, w_ref, b_ref, u_ref, v_ref, g_ref, h_scr, a_scr,
                 *, n_u, n_a):
    j = pl.program_id(1)

    @pl.when(j == 0)
    def _():
        h = _ln(x_ref[...]) * (1.0 + scale_ref[0]) + shift_ref[0]
        h_scr[...] = h.astype(BF16)

    p = _dot(h_scr[...], w_ref[...]) + b_ref[...]

    @pl.when(j < n_u)
    def _():
        for q in range(u_ref.shape[0]):
            u_ref[q] = p[:, q * LANES:(q + 1) * LANES]

    @pl.when((j >= n_u) & (j < n_u + n_a))
    def _():
        a_scr[j - n_u] = p

    @pl.when((j >= n_u + n_a) & (j < n_u + 2 * n_a))
    def _():
        v_ref[...] = a_scr[j - n_u - n_a] * jax.nn.sigmoid(p)

    @pl.when(j >= n_u + 2 * n_a)
    def _():
        g_ref[...] = jax.nn.sigmoid(p).astype(BF16)


def _proj(x2, shift, scale, w_bf, b, s5_width, conv_ch, seq, tm, tn):
    t, d = x2.shape
    n = w_bf.shape[1]
    n_u, n_a = s5_width // tn, conv_ch // tn
    assert n_u == 1, "the S5 input must be one column tile"
    n_lt = s5_width // LANES
    n_g = (n - s5_width - 2 * conv_ch) // tn
    tiles_per_batch = seq // tm
    g0 = n_u + 2 * n_a
    kern = functools.partial(_proj_kernel, n_u=n_u, n_a=n_a)
    return pl.pallas_call(
        kern,
        grid=(t // tm, n // tn),
        in_specs=[pl.BlockSpec((tm, d), lambda i, j: (i, 0)),
                  pl.BlockSpec((1, 1, d), lambda i, j: (i // tiles_per_batch, 0, 0)),
                  pl.BlockSpec((1, 1, d), lambda i, j: (i // tiles_per_batch, 0, 0)),
                  pl.BlockSpec((d, tn), lambda i, j: (0, j)),
                  pl.BlockSpec((1, tn), lambda i, j: (0, j))],
        out_specs=[pl.BlockSpec((n_lt, tm, LANES), lambda i, j: (0, i, 0)),
                   pl.BlockSpec((tm, tn), lambda i, j: (i, jnp.clip(j - n_u - n_a, 0, n_a - 1))),
                   pl.BlockSpec((tm, tn), lambda i, j: (i, jnp.clip(j - g0, 0, n_g - 1)))],
        out_shape=[jax.ShapeDtypeStruct((n_lt, t, LANES), F32),
                   jax.ShapeDtypeStruct((t, conv_ch), F32),
                   jax.ShapeDtypeStruct((t, n_g * tn), BF16)],
        scratch_shapes=[pltpu.VMEM((tm, d), BF16), pltpu.VMEM((n_a, tm, tn), F32)],
        compiler_params=_cparams("arbitrary", "arbitrary"),
        name="proj",
    )(x2, shift, scale, w_bf, b.reshape(1, n))


def _s5_tables(a_re, a_im, log_dt, b_re, b_im, c_re, c_im, d_skip, chunk):
    hp = lax.Precision.HIGHEST
    g, p = a_re.shape
    hw = b_re.shape[-1]
    w = chunk * hw
    dt = jnp.exp(log_dt.astype(F32))[:, None, None]
    lr, li = a_re.astype(F32)[:, :, None], a_im.astype(F32)[:, :, None]

    def power(k):
        mag = jnp.exp(k * lr * dt)
        return mag * jnp.cos(k * li * dt), mag * jnp.sin(k * li * dt)

    a1_re, a1_im = power(1.0)
    den = lr * lr + li * li
    nr, ni = a1_re - 1.0, a1_im
    z_re = (nr * lr + ni * li) / den
    z_im = (ni * lr - nr * li) / den
    br, bi = b_re.astype(F32), b_im.astype(F32)
    bb_re = z_re * br - z_im * bi
    bb_im = z_re * bi + z_im * br
    lag = (jnp.arange(w + hw) // hw).astype(F32)[None, None, :]
    pw_re, pw_im = power(lag)
    ct_re = jnp.tile(c_re.astype(F32).transpose(0, 2, 1), (1, 1, chunk + 1))
    ct_im = jnp.tile(c_im.astype(F32).transpose(0, 2, 1), (1, 1, chunk + 1))
    ca_re = ct_re * pw_re - ct_im * pw_im
    ca_im = ct_re * pw_im + ct_im * pw_re
    kt = (jnp.einsum('gpj,gpl->gjl', bb_re, ca_re[:, :, :w], precision=hp)
          - jnp.einsum('gpj,gpl->gjl', bb_im, ca_im[:, :, :w], precision=hp))
    m = jnp.stack([jnp.pad(kt[:, :, :w - s * hw], ((0, 0), (0, 0), (s * hw, 0))) for s in range(chunk)],
                  axis=1).reshape(g, w, w)
    rev = (chunk - 1 - jnp.arange(w) // hw).astype(F32)[None, None, :]
    rv_re, rv_im = power(rev)
    bt_re, bt_im = jnp.tile(bb_re, (1, 1, chunk)), jnp.tile(bb_im, (1, 1, chunk))
    pt_re = rv_re * bt_re - rv_im * bt_im
    pt_im = rv_re * bt_im + rv_im * bt_re
    qo_re, qo_im = ca_re[:, :, hw:], -ca_im[:, :, hw:]
    al_re, al_im = power(float(chunk))
    al_re, al_im = al_re.reshape(g, 1, p), al_im.reshape(g, 1, p)
    d_t = jnp.tile(d_skip.astype(F32)[:, None, :], (1, 1, chunk))
    return m, pt_re, pt_im, qo_re, qo_im, al_re, al_im, d_t


def _s5_kernel(u_ref, m_ref, pre_ref, pim_ref, qre_ref, qim_ref, are_ref, aim_ref, d_ref, y_ref,
               zre, zim, sre, sim, *, bsz):
    u = u_ref[0]
    rows = u.shape[0]
    nc = rows // bsz
    y = _dot(u, m_ref[0]) + d_ref[0] * u
    nt = (((1,), (1,)), ((), ()))
    zre[...] = lax.dot_general(u, pre_ref[0], nt, preferred_element_type=F32)
    zim[...] = lax.dot_general(u, pim_ref[0], nt, preferred_element_type=F32)
    ar, ai = are_ref[0], aim_ref[0]
    npair = zre.shape[1]

    def step(c, carry):
        nxt = []
        for b in range(bsz):
            sr, si = carry[2 * b], carry[2 * b + 1]
            row = pl.ds(b * nc + c, 1)
            sre[row, :] = sr
            sim[row, :] = si
            nxt.append(ar * sr - ai * si + zre[row, :])
            nxt.append(ar * si + ai * sr + zim[row, :])
        return tuple(nxt)

    zero = jnp.zeros((1, npair), F32)
    lax.fori_loop(0, nc, step, tuple(zero for _ in range(2 * bsz)))
    y = y + _dot(sre[...], qre_ref[0]) + _dot(sim[...], qim_ref[0])
    y_ref[0] = jax.nn.gelu(y, approximate=True)


def _pack_kernel(u_ref, o_ref, *, chunk):
    g, cb, _ = o_ref.shape
    n_lt = u_ref.shape[0]
    per = g // n_lt
    hw = LANES // per
    for s in range(chunk):
        for q in range(n_lt):
            rows = u_ref[q, pl.ds(s, cb, stride=chunk), :]
            for i in range(per):
                o_ref[q * per + i, :, s * hw:(s + 1) * hw] = rows[:, i * hw:(i + 1) * hw]


def _unpack_kernel(y_ref, o_ref, tmp, *, chunk):
    g, cb, _ = y_ref.shape
    n_lt = o_ref.shape[0]
    per = g // n_lt
    hw = LANES // per
    for s in range(chunk):
        for q in range(n_lt):
            for i in range(per):
                tmp[:, i * hw:(i + 1) * hw] = y_ref[q * per + i, :, s * hw:(s + 1) * hw]
            o_ref[q, pl.ds(s, cb, stride=chunk), :] = tmp[...]


def _s5(u, tables, bsz, chunk):
    m, p_re, p_im, q_re, q_im, a_re, a_im, d_t = tables
    n_lt, t, _ = u.shape
    g, cw, _ = m.shape
    npair = p_re.shape[1]
    rows = t // chunk
    cb = min(64, rows)
    by_chunk = pl.BlockSpec((n_lt, cb * chunk, LANES), lambda i: (0, i, 0))
    by_group = pl.BlockSpec((g, cb, cw), lambda i: (0, i, 0))
    ut = pl.pallas_call(
        functools.partial(_pack_kernel, chunk=chunk), grid=(rows // cb,), in_specs=[by_chunk],
        out_specs=by_group, out_shape=jax.ShapeDtypeStruct((g, rows, cw), F32),
        compiler_params=_cparams("arbitrary"), name="s5_pack",
    )(u)
    gspec = lambda *shape: pl.BlockSpec((1,) + shape, lambda i: (i, 0, 0))
    yt = pl.pallas_call(
        functools.partial(_s5_kernel, bsz=bsz),
        grid=(g,),
        in_specs=[gspec(rows, cw), gspec(cw, cw), gspec(npair, cw), gspec(npair, cw),
                  gspec(npair, cw), gspec(npair, cw), gspec(1, npair), gspec(1, npair), gspec(1, cw)],
        out_specs=gspec(rows, cw),
        out_shape=jax.ShapeDtypeStruct((g, rows, cw), F32),
        scratch_shapes=[pltpu.VMEM((rows, npair), F32) for _ in range(4)],
        compiler_params=_cparams("arbitrary"),
        name="s5",
    )(ut, m, p_re, p_im, q_re, q_im, a_re, a_im, d_t)
    return pl.pallas_call(
        functools.partial(_unpack_kernel, chunk=chunk), grid=(rows // cb,), in_specs=[by_group],
        out_specs=by_chunk, out_shape=jax.ShapeDtypeStruct((n_lt, t, LANES), F32),
        scratch_shapes=[pltpu.VMEM((cb, LANES), F32)],
        compiler_params=_cparams("arbitrary"), name="s5_unpack",
    )(yt)


def _conv_kernel(v_ref, w_ref, b_ref, g_ref, beta_ref, o_ref, win, *, span):
    ts = v_ref.shape[1]
    sub = 8

    @pl.when(pl.program_id(1) == 0)
    def _():
        win[0:CONV_HALO, :] = jnp.zeros((CONV_HALO, win.shape[1]), F32)
        win[CONV_HALO + ts:CONV_HALO + ts + sub, :] = jnp.zeros((sub, win.shape[1]), F32)

    win[CONV_HALO:CONV_HALO + ts, :] = v_ref[0]
    base = CONV_HALO - (span - 1)
    acc = None
    for q in range(sub):
        part = None
        for k in range(span):
            if (base + k) % sub == q:
                lo = base + k - q
                term = win[lo:lo + ts + sub, :] * w_ref[k:k + 1, :]
                part = term if part is None else part + term
        if part is not None:
            part = part[q:q + ts, :]
            acc = part if acc is None else acc + part
    y = _ln(acc + b_ref[...]) * g_ref[...] + beta_ref[...]
    o_ref[0] = y * jax.nn.sigmoid(y)
    win[0:CONV_HALO, :] = win[ts:ts + CONV_HALO, :]


def _conv(v, w_dw, b_dw, ln_g, ln_b, ts):
    bsz, seq, ch = v.shape
    span = w_dw.shape[0]
    row = lambda a: a.reshape(1, ch)
    return pl.pallas_call(
        functools.partial(_conv_kernel, span=span),
        grid=(bsz, seq // ts),
        in_specs=[pl.BlockSpec((1, ts, ch), lambda b, s: (b, s, 0)),
                  pl.BlockSpec((span, ch), lambda b, s: (0, 0)),
                  pl.BlockSpec((1, ch), lambda b, s: (0, 0)),
                  pl.BlockSpec((1, ch), lambda b, s: (0, 0)),
                  pl.BlockSpec((1, ch), lambda b, s: (0, 0))],
        out_specs=pl.BlockSpec((1, ts, ch), lambda b, s: (b, s, 0)),
        out_shape=jax.ShapeDtypeStruct((bsz, seq, ch), F32),
        scratch_shapes=[pltpu.VMEM((CONV_HALO + ts + 8, ch), F32)],
        compiler_params=_cparams("arbitrary", "arbitrary"),
        name="conv",
    )(v, w_dw.reshape(span, ch), row(b_dw), row(ln_g), row(ln_b))


def _branch_kernel(ys_ref, ca_ref, gs_ref, wgate_ref, wup_ref, wco_ref, o_ref):
    ys = jnp.concatenate([ys_ref[q] for q in range(ys_ref.shape[0])], axis=-1)
    glu = ys * jax.nn.sigmoid(_dot(ys.astype(BF16), wgate_ref[...]))
    y_s5 = _dot(glu.astype(BF16), wup_ref[...])
    y_conv = _dot(ca_ref[...].astype(BF16), wco_ref[...])
    d = y_s5.shape[1]
    o_ref[...] = gs_ref[:, :d].astype(F32) * y_s5 + gs_ref[:, d:].astype(F32) * y_conv


def _branch(ys, cact, gates, w_gate, w_up, w_co, tm):
    n_lt, t, _ = ys.shape
    cc = cact.shape[1]
    d = w_up.shape[1]
    full = lambda a: pl.BlockSpec(a.shape, lambda i: (0, 0))
    return pl.pallas_call(
        _branch_kernel,
        grid=(t // tm,),
        in_specs=[pl.BlockSpec((n_lt, tm, LANES), lambda i: (0, i, 0)),
                  pl.BlockSpec((tm, cc), lambda i: (i, 0)),
                  pl.BlockSpec((tm, 2 * d), lambda i: (i, 0)),
                  full(w_gate), full(w_up), full(w_co)],
        out_specs=pl.BlockSpec((tm, d), lambda i: (i, 0)),
        out_shape=jax.ShapeDtypeStruct((t, d), F32),
        compiler_params=_cparams("arbitrary"),
        name="branch",
    )(ys, cact, gates, w_gate, w_up, w_co)


def _mix_kernel(m_ref, x_ref, gate_ref, g1_ref, b1_ref, scale2_ref, shift2_ref, wout_ref, wr_ref, br_ref,
                x1_ref, h2_ref, route_ref, cnt_ref, run, *, alpha, n_exp, n_grp):
    @pl.when(pl.program_id(0) == 0)
    def _():
        run[...] = jnp.zeros_like(run)

    mix = _dot(m_ref[...].astype(BF16), wout_ref[...])
    x1 = _ln(alpha * x_ref[...] + (1.0 + gate_ref[0]) * mix) * g1_ref[...] + b1_ref[...]
    x1_ref[...] = x1
    h2 = _ln(x1) * (1.0 + scale2_ref[0]) + shift2_ref[0]
    _store_rows(h2_ref, 0, h2.shape[0], h2.shape[1] // (2 * LANES), _pack_pairs(h2))

    logits = _dot(h2, wr_ref[...]) + br_ref[...]
    tm = logits.shape[0]
    per = n_exp // n_grp
    lane = lax.broadcasted_iota(jnp.int32, logits.shape, 1)
    big = jnp.int32(LANES)
    neg = jnp.float32(-jnp.inf)

    def first_max(val):
        top = jnp.max(val, axis=-1, keepdims=True)
        return top, jnp.min(jnp.where(val == top, lane, big), axis=-1, keepdims=True)

    gmask = (lane >= n_exp) & (lane < n_exp + n_grp)
    gtop, glane = first_max(jnp.where(gmask, logits, neg))
    grp = glane - n_exp
    grp_w = 1.0 / jnp.sum(jnp.where(gmask, jnp.exp(logits - gtop), 0.0), axis=-1, keepdims=True)
    emask = (lane >= grp * per) & (lane < grp * per + per)
    el = jnp.where(emask, logits, neg)
    t1, e1 = first_max(el)
    t2, e2 = first_max(jnp.where(lane == e1, neg, el))
    ex = jnp.exp(t2 - t1)
    w1 = grp_w / (1.0 + ex)
    w2 = grp_w * ex / (1.0 + ex)

    oh1 = (lane == e1).astype(F32)
    oh2 = (lane == e2).astype(F32)
    both = oh1 + oh2
    r_i = lax.broadcasted_iota(jnp.int32, (tm, tm), 0)
    c_i = lax.broadcasted_iota(jnp.int32, (tm, tm), 1)
    before = _dot((r_i > c_i).astype(F32), both) + run[...]
    rank1 = jnp.sum(before * oh1, axis=-1, keepdims=True)
    rank2 = jnp.sum(before * oh2, axis=-1, keepdims=True)
    run[...] = run[...] + jnp.sum(both, axis=0, keepdims=True)
    cnt_ref[...] = run[...]

    route = jnp.where(lane == 0, e1.astype(F32), 0.0)
    route = jnp.where(lane == 1, e2.astype(F32), route)
    route = jnp.where(lane == 2, w1, route)
    route = jnp.where(lane == 3, w2, route)
    route = jnp.where(lane == 4, rank1, route)
    route = jnp.where(lane == 5, rank2, route)
    route_ref[...] = route


def _mix(merged, x2, gate1, g1, b1, scale2, shift2, w_out, w_r, b_r, seq, tm, alpha, n_exp, n_grp):
    t, d = x2.shape
    k = d // (2 * LANES)
    tiles_per_batch = seq // tm
    tile = pl.BlockSpec((tm, d), lambda i: (i, 0))
    per_batch = pl.BlockSpec((1, 1, d), lambda i: (i // tiles_per_batch, 0, 0))
    const = lambda a: pl.BlockSpec(a.shape, lambda i: (0, 0))
    g1, b1 = g1.reshape(1, d), b1.reshape(1, d)
    kern = functools.partial(_mix_kernel, alpha=alpha, n_exp=n_exp, n_grp=n_grp)
    return pl.pallas_call(
        kern,
        grid=(t // tm,),
        in_specs=[tile, tile, per_batch, const(g1), const(b1), per_batch, per_batch,
                  const(w_out), const(w_r), const(b_r)],
        out_specs=[tile, pl.BlockSpec((tm * k, LANES), lambda i: (i, 0)),
                   pl.BlockSpec((tm, LANES), lambda i: (i, 0)),
                   pl.BlockSpec((1, LANES), lambda i: (0, 0))],
        out_shape=[jax.ShapeDtypeStruct((t, d), F32), jax.ShapeDtypeStruct((t * k, LANES), jnp.uint32),
                   jax.ShapeDtypeStruct((t, LANES), F32), jax.ShapeDtypeStruct((1, LANES), F32)],
        scratch_shapes=[pltpu.VMEM((1, LANES), F32)],
        compiler_params=_cparams("arbitrary"),
        name="mix_route",
    )(merged, x2, gate1, g1, b1, scale2, shift2, w_out, w_r, b_r)


def _moe_kernel(bexp_ref, nused_ref, tok_ref, tok_next_ref, slot_ref, h_hbm, wg_ref, wu_ref, wd_ref, y_hbm,
                xbuf, ybuf, gsem, ssem):
    del bexp_ref
    rows = tok_ref.shape[2]
    k = xbuf.shape[1] // rows
    span = rows * k
    b = pl.program_id(0)
    n_used = nused_ref[0]
    cur, nxt = b % 2, (b + 1) % 2

    def gather_start(idx_ref, buf):
        for r in range(rows):
            src = h_hbm.at[pl.ds(pl.multiple_of(idx_ref[0, 0, r], k), k)]
            pltpu.make_async_copy(src, xbuf.at[buf, pl.ds(r * k, k)], gsem.at[buf]).start()

    def gather_wait(buf):
        pltpu.make_async_copy(h_hbm.at[pl.ds(0, span)], xbuf.at[buf], gsem.at[buf]).wait()

    def scatter_wait(buf):
        pltpu.make_async_copy(ybuf.at[buf], y_hbm.at[pl.ds(0, span)], ssem.at[buf]).wait()

    @pl.when(b == 0)
    def _():
        ybuf[1] = jnp.zeros(ybuf.shape[1:], ybuf.dtype)
        spare = pltpu.make_async_copy(ybuf.at[1], y_hbm.at[pl.ds(y_hbm.shape[0] - span, span)], ssem.at[1])
        spare.start()
        spare.wait()
        gather_start(tok_ref, 0)

    @pl.when((b >= 2) & (b < n_used))
    def _():
        scatter_wait(cur)

    @pl.when(b < n_used)
    def _():
        gather_start(tok_next_ref, nxt)
        gather_wait(cur)
        x = _unpack_pairs(_load_rows(xbuf.at[cur], 0, rows, k, k))
        hg = _dot(x, wg_ref[0])
        act = hg * jax.nn.sigmoid(hg) * _dot(x, wu_ref[0])
        _store_rows(ybuf.at[cur], 0, rows, k, _pack_pairs(_dot(act, wd_ref[0])))
        for r in range(rows):
            dst = y_hbm.at[pl.ds(pl.multiple_of(slot_ref[0, 0, r], k), k)]
            pltpu.make_async_copy(ybuf.at[cur, pl.ds(r * k, k)], dst, ssem.at[cur]).start()

    @pl.when(b == n_used - 1)
    def _():
        gather_wait(nxt)
        scatter_wait(cur)

        @pl.when(b >= 1)
        def _():
            scatter_wait(nxt)


def _moe(h2, row_tok, row_slot, block_expert, n_used, w_g, w_u, w_d, n_slots):
    de, d = w_d.shape[-2:]
    k = d // (2 * LANES)
    n_blocks = row_tok.shape[0]
    smem = lambda off: pl.BlockSpec((1, 1, MOE_ROWS), lambda b, e, n: (jnp.minimum(b + off, n_blocks - 1), 0, 0),
                                    memory_space=pltpu.SMEM)
    grid_spec = pltpu.PrefetchScalarGridSpec(
        num_scalar_prefetch=2,
        grid=(n_blocks,),
        in_specs=[smem(0), smem(1), smem(0),
                  pl.BlockSpec(memory_space=pl.ANY),
                  pl.BlockSpec((1, d, de), lambda b, e, n: (e[b], 0, 0)),
                  pl.BlockSpec((1, d, de), lambda b, e, n: (e[b], 0, 0)),
                  pl.BlockSpec((1, de, d), lambda b, e, n: (e[b], 0, 0))],
        out_specs=pl.BlockSpec(memory_space=pl.ANY),
        scratch_shapes=[pltpu.VMEM((2, MOE_ROWS * k, LANES), jnp.uint32),
                        pltpu.VMEM((2, MOE_ROWS * k, LANES), jnp.uint32),
                        pltpu.SemaphoreType.DMA((2,)), pltpu.SemaphoreType.DMA((2,))],
    )
    return pl.pallas_call(
        _moe_kernel,
        grid_spec=grid_spec,
        out_shape=jax.ShapeDtypeStruct(((n_slots + MOE_ROWS) * k, LANES), jnp.uint32),
        compiler_params=_cparams("arbitrary"),
        name="moe",
    )(block_expert, n_used, row_tok, row_tok, row_slot, h2, w_g, w_u, w_d)


def _dispatch_tables(route, counts, n_exp, t, k):
    n_slots = t * TOP_K
    n_blocks = n_slots // MOE_ROWS + n_exp
    slot_e = route[:, 0:TOP_K].astype(jnp.int32)
    rank = route[:, 4:4 + TOP_K].astype(jnp.int32)
    cnt = counts[0, :n_exp].astype(jnp.int32)
    padded = (cnt + MOE_ROWS - 1) // MOE_ROWS * MOE_ROWS
    pends = jnp.cumsum(padded)
    dest = ((pends - padded)[slot_e] + rank).reshape(-1)
    spare = n_slots + jnp.arange(n_blocks * MOE_ROWS, dtype=jnp.int32) % MOE_ROWS
    row_slot = spare.at[dest].set(jnp.arange(n_slots, dtype=jnp.int32))
    row_tok = jnp.where(row_slot < n_slots, row_slot // TOP_K, 0)
    block_start = jnp.arange(n_blocks, dtype=jnp.int32) * MOE_ROWS
    block_expert = jnp.minimum(jnp.searchsorted(pends, block_start, side='right'), n_exp - 1).astype(jnp.int32)
    n_used = (pends[-1:] // MOE_ROWS).astype(jnp.int32)
    shape3 = (n_blocks, 1, MOE_ROWS)
    return (row_tok * k).reshape(shape3), (row_slot * k).reshape(shape3), block_expert, n_used


def _final_kernel(y_ref, route_ref, x1_ref, gate_ref, g_ref, b_ref, o_ref, *, alpha):
    tm, d = x1_ref.shape
    k = d // (2 * LANES)
    route = route_ref[...]
    y0 = _unpack_pairs(_load_rows(y_ref, 0, tm, k, TOP_K * k))
    y1 = _unpack_pairs(_load_rows(y_ref, k, tm, k, TOP_K * k))
    ffn = route[:, 2:3] * y0 + route[:, 3:4] * y1
    o_ref[...] = _ln(alpha * x1_ref[...] + (1.0 + gate_ref[0]) * ffn) * g_ref[...] + b_ref[...]


def _final(y_slots, route, x1, gate2, g2, b2, seq, tm, alpha):
    t, d = x1.shape
    k = d // (2 * LANES)
    tiles_per_batch = seq // tm
    return pl.pallas_call(
        functools.partial(_final_kernel, alpha=alpha),
        grid=(t // tm,),
        in_specs=[pl.BlockSpec((tm * TOP_K * k, LANES), lambda i: (i, 0)),
                  pl.BlockSpec((tm, LANES), lambda i: (i, 0)),
                  pl.BlockSpec((tm, d), lambda i: (i, 0)),
                  pl.BlockSpec((1, 1, d), lambda i: (i // tiles_per_batch, 0, 0)),
                  pl.BlockSpec((1, d), lambda i: (0, 0)),
                  pl.BlockSpec((1, d), lambda i: (0, 0))],
        out_specs=pl.BlockSpec((tm, d), lambda i: (i, 0)),
        out_shape=jax.ShapeDtypeStruct((t, d), F32),
        compiler_params=_cparams("arbitrary"),
        name="final",
    )(y_slots, route, x1, gate2, g2.reshape(1, d), b2.reshape(1, d))


def _tiles(seq):
    return dict(proj=min(1024, seq), conv=min(512, seq), branch=min(256, seq), mix=min(256, seq),
                final=min(512, seq))


def kernel(x, c, w_ada, b_ada, w_in, b_in, s5_a_re, s5_a_im, s5_log_dt, s5_b_re, s5_b_im, s5_c_re, s5_c_im, s5_d, w_s5_gate, w_s5_up, conv_dw, conv_dw_b, conv_ln_g, conv_ln_b, w_conv_out, w_out, ln1_g, ln1_b, w_route_group, b_route_group, w_route_expert, b_route_expert, w_exp_gate, w_exp_up, w_exp_down, ln2_g, ln2_b):
    bsz, seq, d = x.shape
    t = bsz * seq
    depth = w_ada.shape[0]
    alpha = (2.0 * depth) ** 0.25
    s5_width = w_s5_gate.shape[1]
    conv_ch = conv_dw.shape[-1]
    n_grp = w_route_group.shape[-1]
    n_exp = w_route_expert.shape[-1]
    tl = _tiles(seq)
    tn = min(512, s5_width)
    x2 = x.reshape(t, d)
    for l in range(depth):
        mod = _ada(c, w_ada[l], b_ada[l])
        shift1, scale1, gate1, shift2, scale2, gate2 = [
            m.reshape(bsz, 1, d) for m in jnp.split(mod, 6, axis=-1)]

        u, v, gates = _proj(x2, shift1, scale1, w_in[l].astype(BF16), b_in[l], s5_width, conv_ch, seq,
                            tl["proj"], tn)
        tables = _s5_tables(s5_a_re[l], s5_a_im[l], s5_log_dt[l], s5_b_re[l], s5_b_im[l],
                            s5_c_re[l], s5_c_im[l], s5_d[l], S5_CHUNK)
        ys = _s5(u, tables, bsz, S5_CHUNK)
        cact = _conv(v.reshape(bsz, seq, conv_ch), conv_dw[l], conv_dw_b[l], conv_ln_g[l], conv_ln_b[l],
                     tl["conv"]).reshape(t, conv_ch)
        merged = _branch(ys, cact, gates, w_s5_gate[l].astype(BF16), w_s5_up[l].astype(BF16),
                         w_conv_out[l].astype(BF16), tl["branch"])

        w_r = jnp.zeros((d, LANES), F32).at[:, :n_exp].set(w_route_expert[l])
        w_r = w_r.at[:, n_exp:n_exp + n_grp].set(w_route_group[l])
        b_r = jnp.zeros((1, LANES), F32).at[0, :n_exp].set(b_route_expert[l])
        b_r = b_r.at[0, n_exp:n_exp + n_grp].set(b_route_group[l])
        x1, h2, route, counts = _mix(merged, x2, gate1, ln1_g[l], ln1_b[l], scale2, shift2,
                                     w_out[l].astype(BF16), w_r, b_r, seq, tl["mix"], alpha, n_exp, n_grp)
        row_tok, row_slot, block_expert, n_used = _dispatch_tables(route, counts, n_exp, t, d // (2 * LANES))
        y_slots = _moe(h2, row_tok, row_slot, block_expert, n_used, w_exp_gate[l], w_exp_up[l],
                       w_exp_down[l], t * TOP_K)
        x2 = _final(y_slots, route, x1, gate2, ln2_g[l], ln2_b[l], seq, tl["final"], alpha)
    return x2.reshape(bsz, seq, d)
```

```python
import functools
import math

import jax
import jax.numpy as jnp
from jax import lax
from jax.experimental import pallas as pl
from jax.experimental.pallas import tpu as pltpu

F32 = jnp.float32
BF16 = jnp.bfloat16
LN_EPS = 1e-5
TOP_K = 2
LANES = 128
S5_CHUNK = 16
CONV_HALO = 32
MOE_ROWS = 256
VMEM_LIMIT = 56 * 1024 * 1024


def _cparams(*sem):
    return pltpu.CompilerParams(dimension_semantics=sem, vmem_limit_bytes=VMEM_LIMIT)


def _ln(x):
    mu = jnp.mean(x, axis=-1, keepdims=True)
    xc = x - mu
    var = jnp.mean(xc * xc, axis=-1, keepdims=True)
    return xc * lax.rsqrt(var + LN_EPS)


def _dot(a, b):
    return jnp.dot(a, b, preferred_element_type=F32)


def _sigmoid(x):
    return 0.5 * jnp.tanh(0.5 * x) + 0.5


def _pack_pairs(a):
    w = a.shape[1] // 2
    hi = lax.bitcast_convert_type(a[:, :w].astype(BF16).astype(F32), jnp.uint32)
    lo = lax.bitcast_convert_type(a[:, w:].astype(BF16).astype(F32), jnp.uint32)
    return hi | (lo >> 16)


def _unpack_pairs(words):
    hi = lax.bitcast_convert_type(words & jnp.uint32(0xFFFF0000), F32)
    lo = lax.bitcast_convert_type(words << 16, F32)
    return jnp.concatenate([hi, lo], axis=-1)


def _store_rows(ref, first, n, k, packed):
    for j in range(k):
        ref[pl.ds(first + j, n, stride=k), :] = packed[:, j * LANES:(j + 1) * LANES]


def _load_rows(ref, first, n, k, stride):
    return jnp.concatenate([ref[pl.ds(first + j, n, stride=stride), :] for j in range(k)], axis=-1)


def _ada_kernel(c_ref, w_ref, b_ref, o_ref):
    c = c_ref[...]
    o_ref[...] = _dot(c * jax.nn.sigmoid(c), w_ref[...]) + b_ref[...]


def _ada(c, w, b, tn=1024):
    bsz, d = c.shape
    n = w.shape[1]
    rows = 8
    cp = jnp.zeros((rows, d), F32).at[:bsz].set(c)
    out = pl.pallas_call(
        _ada_kernel,
        grid=(n // tn,),
        in_specs=[pl.BlockSpec((rows, d), lambda j: (0, 0)),
                  pl.BlockSpec((d, tn), lambda j: (0, j)),
                  pl.BlockSpec((1, tn), lambda j: (0, j))],
        out_specs=pl.BlockSpec((rows, tn), lambda j: (0, j)),
        out_shape=jax.ShapeDtypeStruct((rows, n), F32),
        compiler_params=_cparams("arbitrary"),
        name="ada",
    )(cp, w, b.reshape(1, n))
    return out[:bsz]


def _proj_kernel(x_ref, shift_ref, scale_ref, w_ref, b_ref, u_ref, v_ref, g_ref, h_scr, a_scr,
                 *, n_u, n_a):
    j = pl.program_id(1)

    @pl.when(j == 0)
    def _():
        h = _ln(x_ref[...]) * (1.0 + scale_ref[0]) + shift_ref[0]
        h_scr[...] = h.astype(BF16)

    p = _dot(h_scr[...], w_ref[...]) + b_ref[...]

    @pl.when(j < n_u)
    def _():
        for q in range(u_ref.shape[0]):
            u_ref[q] = p[:, q * LANES:(q + 1) * LANES]

    @pl.when((j >= n_u) & (j < n_u + n_a))
    def _():
        a_scr[j - n_u] = p

    @pl.when((j >= n_u + n_a) & (j < n_u + 2 * n_a))
    def _():
        v_ref[...] = a_scr[j - n_u - n_a] * _sigmoid(p)

    @pl.when(j >= n_u + 2 * n_a)
    def _():
        g_ref[...] = _sigmoid(p).astype(BF16)


def _proj(x2, shift, scale, w_bf, b, s5_width, conv_ch, seq, tm, tn):
    t, d = x2.shape
    n = w_bf.shape[1]
    n_u, n_a = s5_width // tn, conv_ch // tn
    assert n_u == 1, "the S5 input must be one column tile"
    n_lt = s5_width // LANES
    n_g = (n - s5_width - 2 * conv_ch) // tn
    tiles_per_batch = seq // tm
    g0 = n_u + 2 * n_a
    kern = functools.partial(_proj_kernel, n_u=n_u, n_a=n_a)
    return pl.pallas_call(
        kern,
        grid=(t // tm, n // tn),
        in_specs=[pl.BlockSpec((tm, d), lambda i, j: (i, 0)),
                  pl.BlockSpec((1, 1, d), lambda i, j: (i // tiles_per_batch, 0, 0)),
                  pl.BlockSpec((1, 1, d), lambda i, j: (i // tiles_per_batch, 0, 0)),
                  pl.BlockSpec((d, tn), lambda i, j: (0, j)),
                  pl.BlockSpec((1, tn), lambda i, j: (0, j))],
        out_specs=[pl.BlockSpec((n_lt, tm, LANES), lambda i, j: (0, i, 0)),
                   pl.BlockSpec((tm, tn), lambda i, j: (i, jnp.clip(j - n_u - n_a, 0, n_a - 1))),
                   pl.BlockSpec((tm, tn), lambda i, j: (i, jnp.clip(j - g0, 0, n_g - 1)))],
        out_shape=[jax.ShapeDtypeStruct((n_lt, t, LANES), F32),
                   jax.ShapeDtypeStruct((t, conv_ch), F32),
                   jax.ShapeDtypeStruct((t, n_g * tn), BF16)],
        scratch_shapes=[pltpu.VMEM((tm, d), BF16), pltpu.VMEM((n_a, tm, tn), F32)],
        compiler_params=_cparams("arbitrary", "arbitrary"),
        name="proj",
    )(x2, shift, scale, w_bf, b.reshape(1, n))


def _s5_tables(a_re, a_im, log_dt, b_re, b_im, c_re, c_im, d_skip, chunk):
    hp = lax.Precision.HIGHEST
    g, p = a_re.shape
    hw = b_re.shape[-1]
    w = chunk * hw
    dt = jnp.exp(log_dt.astype(F32))[:, None, None]
    lr, li = a_re.astype(F32)[:, :, None], a_im.astype(F32)[:, :, None]

    def power(k):
        mag = jnp.exp(k * lr * dt)
        return mag * jnp.cos(k * li * dt), mag * jnp.sin(k * li * dt)

    a1_re, a1_im = power(1.0)
    den = lr * lr + li * li
    nr, ni = a1_re - 1.0, a1_im
    z_re = (nr * lr + ni * li) / den
    z_im = (ni * lr - nr * li) / den
    br, bi = b_re.astype(F32), b_im.astype(F32)
    bb_re = z_re * br - z_im * bi
    bb_im = z_re * bi + z_im * br
    lag = (jnp.arange(w + hw) // hw).astype(F32)[None, None, :]
    pw_re, pw_im = power(lag)
    ct_re = jnp.tile(c_re.astype(F32).transpose(0, 2, 1), (1, 1, chunk + 1))
    ct_im = jnp.tile(c_im.astype(F32).transpose(0, 2, 1), (1, 1, chunk + 1))
    ca_re = ct_re * pw_re - ct_im * pw_im
    ca_im = ct_re * pw_im + ct_im * pw_re
    kt = (jnp.einsum('gpj,gpl->gjl', bb_re, ca_re[:, :, :w], precision=hp)
          - jnp.einsum('gpj,gpl->gjl', bb_im, ca_im[:, :, :w], precision=hp))
    m = jnp.stack([jnp.pad(kt[:, :, :w - s * hw], ((0, 0), (0, 0), (s * hw, 0))) for s in range(chunk)],
                  axis=1).reshape(g, w, w)
    rev = (chunk - 1 - jnp.arange(w) // hw).astype(F32)[None, None, :]
    rv_re, rv_im = power(rev)
    bt_re, bt_im = jnp.tile(bb_re, (1, 1, chunk)), jnp.tile(bb_im, (1, 1, chunk))
    pt_re = rv_re * bt_re - rv_im * bt_im
    pt_im = rv_re * bt_im + rv_im * bt_re
    qo_re, qo_im = ca_re[:, :, hw:], -ca_im[:, :, hw:]
    al_re, al_im = power(float(chunk))
    al_re, al_im = al_re.reshape(g, 1, p), al_im.reshape(g, 1, p)
    d_t = jnp.tile(d_skip.astype(F32)[:, None, :], (1, 1, chunk))
    return m, pt_re, pt_im, qo_re, qo_im, al_re, al_im, d_t


def _s5_kernel(u_ref, m_ref, pre_ref, pim_ref, qre_ref, qim_ref, are_ref, aim_ref, d_ref, y_ref,
               zre, zim, sre, sim, *, bsz):
    u = u_ref[0]
    rows = u.shape[0]
    nc = rows // bsz
    y = _dot(u, m_ref[0]) + d_ref[0] * u
    nt = (((1,), (1,)), ((), ()))
    zre[...] = lax.dot_general(u, pre_ref[0], nt, preferred_element_type=F32)
    zim[...] = lax.dot_general(u, pim_ref[0], nt, preferred_element_type=F32)
    ar, ai = are_ref[0], aim_ref[0]
    npair = zre.shape[1]

    def step(c, carry):
        nxt = []
        for b in range(bsz):
            sr, si = carry[2 * b], carry[2 * b + 1]
            row = pl.ds(b * nc + c, 1)
            sre[row, :] = sr
            sim[row, :] = si
            nxt.append(ar * sr - ai * si + zre[row, :])
            nxt.append(ar * si + ai * sr + zim[row, :])
        return tuple(nxt)

    zero = jnp.zeros((1, npair), F32)
    lax.fori_loop(0, nc, step, tuple(zero for _ in range(2 * bsz)))
    y = y + _dot(sre[...], qre_ref[0]) + _dot(sim[...], qim_ref[0])
    y_ref[0] = jax.nn.gelu(y, approximate=True)


def _pack_kernel(u_ref, o_ref, *, chunk):
    g, cb, _ = o_ref.shape
    n_lt = u_ref.shape[0]
    per = g // n_lt
    hw = LANES // per
    for s in range(chunk):
        for q in range(n_lt):
            rows = u_ref[q, pl.ds(s, cb, stride=chunk), :]
            for i in range(per):
                o_ref[q * per + i, :, s * hw:(s + 1) * hw] = rows[:, i * hw:(i + 1) * hw]


def _unpack_kernel(y_ref, o_ref, tmp, *, chunk):
    g, cb, _ = y_ref.shape
    n_lt = o_ref.shape[0]
    per = g // n_lt
    hw = LANES // per
    for s in range(chunk):
        for q in range(n_lt):
            for i in range(per):
                tmp[:, i * hw:(i + 1) * hw] = y_ref[q * per + i, :, s * hw:(s + 1) * hw]
            o_ref[q, pl.ds(s, cb, stride=chunk), :] = tmp[...]


def _s5(u, tables, bsz, chunk):
    m, p_re, p_im, q_re, q_im, a_re, a_im, d_t = tables
    n_lt, t, _ = u.shape
    g, cw, _ = m.shape
    npair = p_re.shape[1]
    rows = t // chunk
    cb = min(64, rows)
    by_chunk = pl.BlockSpec((n_lt, cb * chunk, LANES), lambda i: (0, i, 0))
    by_group = pl.BlockSpec((g, cb, cw), lambda i: (0, i, 0))
    ut = pl.pallas_call(
        functools.partial(_pack_kernel, chunk=chunk), grid=(rows // cb,), in_specs=[by_chunk],
        out_specs=by_group, out_shape=jax.ShapeDtypeStruct((g, rows, cw), F32),
        compiler_params=_cparams("arbitrary"), name="s5_pack",
    )(u)
    gspec = lambda *shape: pl.BlockSpec((1,) + shape, lambda i: (i, 0, 0))
    yt = pl.pallas_call(
        functools.partial(_s5_kernel, bsz=bsz),
        grid=(g,),
        in_specs=[gspec(rows, cw), gspec(cw, cw), gspec(npair, cw), gspec(npair, cw),
                  gspec(npair, cw), gspec(npair, cw), gspec(1, npair), gspec(1, npair), gspec(1, cw)],
        out_specs=gspec(rows, cw),
        out_shape=jax.ShapeDtypeStruct((g, rows, cw), F32),
        scratch_shapes=[pltpu.VMEM((rows, npair), F32) for _ in range(4)],
        compiler_params=_cparams("arbitrary"),
        name="s5",
    )(ut, m, p_re, p_im, q_re, q_im, a_re, a_im, d_t)
    return pl.pallas_call(
        functools.partial(_unpack_kernel, chunk=chunk), grid=(rows // cb,), in_specs=[by_group],
        out_specs=by_chunk, out_shape=jax.ShapeDtypeStruct((n_lt, t, LANES), F32),
        scratch_shapes=[pltpu.VMEM((cb, LANES), F32)],
        compiler_params=_cparams("arbitrary"), name="s5_unpack",
    )(yt)


def _conv_kernel(v_ref, w_ref, b_ref, g_ref, beta_ref, o_ref, win, *, span):
    ts = v_ref.shape[1]
    sub = 8

    @pl.when(pl.program_id(1) == 0)
    def _():
        win[0:CONV_HALO, :] = jnp.zeros((CONV_HALO, win.shape[1]), F32)
        win[CONV_HALO + ts:CONV_HALO + ts + sub, :] = jnp.zeros((sub, win.shape[1]), F32)

    win[CONV_HALO:CONV_HALO + ts, :] = v_ref[0]
    base = CONV_HALO - (span - 1)
    acc = None
    for q in range(sub):
        part = None
        for k in range(span):
            if (base + k) % sub == q:
                lo = base + k - q
                term = win[lo:lo + ts + sub, :] * w_ref[k:k + 1, :]
                part = term if part is None else part + term
        if part is not None:
            part = part[q:q + ts, :]
            acc = part if acc is None else acc + part
    y = _ln(acc + b_ref[...]) * g_ref[...] + beta_ref[...]
    o_ref[0] = y * jax.nn.sigmoid(y)
    win[0:CONV_HALO, :] = win[ts:ts + CONV_HALO, :]


def _conv(v, w_dw, b_dw, ln_g, ln_b, ts):
    bsz, seq, ch = v.shape
    span = w_dw.shape[0]
    row = lambda a: a.reshape(1, ch)
    return pl.pallas_call(
        functools.partial(_conv_kernel, span=span),
        grid=(bsz, seq // ts),
        in_specs=[pl.BlockSpec((1, ts, ch), lambda b, s: (b, s, 0)),
                  pl.BlockSpec((span, ch), lambda b, s: (0, 0)),
                  pl.BlockSpec((1, ch), lambda b, s: (0, 0)),
                  pl.BlockSpec((1, ch), lambda b, s: (0, 0)),
                  pl.BlockSpec((1, ch), lambda b, s: (0, 0))],
        out_specs=pl.BlockSpec((1, ts, ch), lambda b, s: (b, s, 0)),
        out_shape=jax.ShapeDtypeStruct((bsz, seq, ch), F32),
        scratch_shapes=[pltpu.VMEM((CONV_HALO + ts + 8, ch), F32)],
        compiler_params=_cparams("arbitrary", "arbitrary"),
        name="conv",
    )(v, w_dw.reshape(span, ch), row(b_dw), row(ln_g), row(ln_b))


def _branch_kernel(ys_ref, ca_ref, gs_ref, wgate_ref, wup_ref, wco_ref, o_ref):
    ys = jnp.concatenate([ys_ref[q] for q in range(ys_ref.shape[0])], axis=-1)
    glu = ys * jax.nn.sigmoid(_dot(ys.astype(BF16), wgate_ref[...]))
    y_s5 = _dot(glu.astype(BF16), wup_ref[...])
    y_conv = _dot(ca_ref[...].astype(BF16), wco_ref[...])
    d = y_s5.shape[1]
    merged = gs_ref[:, :d].astype(F32) * y_s5 + gs_ref[:, d:].astype(F32) * y_conv
    o_ref[...] = merged.astype(BF16)


def _branch(ys, cact, gates, w_gate, w_up, w_co, tm):
    n_lt, t, _ = ys.shape
    cc = cact.shape[1]
    d = w_up.shape[1]
    full = lambda a: pl.BlockSpec(a.shape, lambda i: (0, 0))
    return pl.pallas_call(
        _branch_kernel,
        grid=(t // tm,),
        in_specs=[pl.BlockSpec((n_lt, tm, LANES), lambda i: (0, i, 0)),
                  pl.BlockSpec((tm, cc), lambda i: (i, 0)),
                  pl.BlockSpec((tm, 2 * d), lambda i: (i, 0)),
                  full(w_gate), full(w_up), full(w_co)],
        out_specs=pl.BlockSpec((tm, d), lambda i: (i, 0)),
        out_shape=jax.ShapeDtypeStruct((t, d), BF16),
        compiler_params=_cparams("arbitrary"),
        name="branch",
    )(ys, cact, gates, w_gate, w_up, w_co)


def _mix_kernel(m_ref, x_ref, gate_ref, g1_ref, b1_ref, scale2_ref, shift2_ref, wout_ref, wr_ref, br_ref,
                x1_ref, h2_ref, route_ref, cnt_ref, run, *, alpha, n_exp, n_grp):
    @pl.when(pl.program_id(0) == 0)
    def _():
        run[...] = jnp.zeros_like(run)

    mix = _dot(m_ref[...], wout_ref[...])
    x1 = _ln(alpha * x_ref[...] + (1.0 + gate_ref[0]) * mix) * g1_ref[...] + b1_ref[...]
    x1_ref[...] = x1
    h2 = _ln(x1) * (1.0 + scale2_ref[0]) + shift2_ref[0]
    _store_rows(h2_ref, 0, h2.shape[0], h2.shape[1] // (2 * LANES), _pack_pairs(h2))

    logits = _dot(h2, wr_ref[...]) + br_ref[...]
    tm = logits.shape[0]
    per = n_exp // n_grp
    lane = lax.broadcasted_iota(jnp.int32, logits.shape, 1)
    big = jnp.int32(LANES)
    neg = jnp.float32(-jnp.inf)

    def first_max(val):
        top = jnp.max(val, axis=-1, keepdims=True)
        return top, jnp.min(jnp.where(val == top, lane, big), axis=-1, keepdims=True)

    gmask = (lane >= n_exp) & (lane < n_exp + n_grp)
    gtop, glane = first_max(jnp.where(gmask, logits, neg))
    grp = glane - n_exp
    grp_w = 1.0 / jnp.sum(jnp.where(gmask, jnp.exp(logits - gtop), 0.0), axis=-1, keepdims=True)
    emask = (lane >= grp * per) & (lane < grp * per + per)
    el = jnp.where(emask, logits, neg)
    t1, e1 = first_max(el)
    t2, e2 = first_max(jnp.where(lane == e1, neg, el))
    ex = jnp.exp(t2 - t1)
    w1 = grp_w / (1.0 + ex)
    w2 = grp_w * ex / (1.0 + ex)

    oh1 = (lane == e1).astype(F32)
    oh2 = (lane == e2).astype(F32)
    both = oh1 + oh2
    r_i = lax.broadcasted_iota(jnp.int32, (tm, tm), 0)
    c_i = lax.broadcasted_iota(jnp.int32, (tm, tm), 1)
    before = _dot((r_i > c_i).astype(F32), both) + run[...]
    rank1 = jnp.sum(before * oh1, axis=-1, keepdims=True)
    rank2 = jnp.sum(before * oh2, axis=-1, keepdims=True)
    run[...] = run[...] + jnp.sum(both, axis=0, keepdims=True)
    cnt_ref[...] = run[...]

    route = jnp.where(lane == 0, e1.astype(F32), 0.0)
    route = jnp.where(lane == 1, e2.astype(F32), route)
    route = jnp.where(lane == 2, w1, route)
    route = jnp.where(lane == 3, w2, route)
    route = jnp.where(lane == 4, rank1, route)
    route = jnp.where(lane == 5, rank2, route)
    route_ref[...] = route


def _mix(merged, x2, gate1, g1, b1, scale2, shift2, w_out, w_r, b_r, seq, tm, alpha, n_exp, n_grp):
    t, d = x2.shape
    k = d // (2 * LANES)
    tiles_per_batch = seq // tm
    tile = pl.BlockSpec((tm, d), lambda i: (i, 0))
    per_batch = pl.BlockSpec((1, 1, d), lambda i: (i // tiles_per_batch, 0, 0))
    const = lambda a: pl.BlockSpec(a.shape, lambda i: (0, 0))
    g1, b1 = g1.reshape(1, d), b1.reshape(1, d)
    kern = functools.partial(_mix_kernel, alpha=alpha, n_exp=n_exp, n_grp=n_grp)
    return pl.pallas_call(
        kern,
        grid=(t // tm,),
        in_specs=[tile, tile, per_batch, const(g1), const(b1), per_batch, per_batch,
                  const(w_out), const(w_r), const(b_r)],
        out_specs=[tile, pl.BlockSpec((tm * k, LANES), lambda i: (i, 0)),
                   pl.BlockSpec((tm, LANES), lambda i: (i, 0)),
                   pl.BlockSpec((1, LANES), lambda i: (0, 0))],
        out_shape=[jax.ShapeDtypeStruct((t, d), F32), jax.ShapeDtypeStruct((t * k, LANES), jnp.uint32),
                   jax.ShapeDtypeStruct((t, LANES), F32), jax.ShapeDtypeStruct((1, LANES), F32)],
        scratch_shapes=[pltpu.VMEM((1, LANES), F32)],
        compiler_params=_cparams("arbitrary"),
        name="mix_route",
    )(merged, x2, gate1, g1, b1, scale2, shift2, w_out, w_r, b_r)


def _moe_kernel(bexp_ref, nused_ref, tok_ref, tok_next_ref, slot_prev_ref, slot_ref, h_hbm, wg_ref, wu_ref,
                wd_ref, y_hbm, xbuf, ybuf, gsem, ssem):
    del bexp_ref
    rows = tok_ref.shape[2]
    k = xbuf.shape[1] // rows
    span = rows * k
    de, d = wd_ref.shape[1:]
    half = d // 2
    piece = 2 * LANES
    b = pl.program_id(0)
    n_used = nused_ref[0]
    cur, nxt = b % 2, (b + 1) % 2

    def gather_row(idx_ref, buf, r):
        src = h_hbm.at[pl.ds(pl.multiple_of(idx_ref[0, 0, r], k), k)]
        pltpu.make_async_copy(src, xbuf.at[buf, pl.ds(r * k, k)], gsem.at[buf]).start()

    def scatter_row(idx_ref, buf, r):
        dst = y_hbm.at[pl.ds(pl.multiple_of(idx_ref[0, 0, r], k), k)]
        pltpu.make_async_copy(ybuf.at[buf, pl.ds(r * k, k)], dst, ssem.at[buf]).start()

    def gather_wait(buf):
        pltpu.make_async_copy(h_hbm.at[pl.ds(0, span)], xbuf.at[buf], gsem.at[buf]).wait()

    def scatter_wait(buf):
        pltpu.make_async_copy(ybuf.at[buf], y_hbm.at[pl.ds(0, span)], ssem.at[buf]).wait()

    @pl.when(b == 0)
    def _():
        ybuf[1] = jnp.zeros(ybuf.shape[1:], ybuf.dtype)
        for r in range(rows):
            gather_row(tok_ref, 0, r)

    @pl.when((b >= 1) & (b < n_used))
    def _():
        scatter_wait(cur)

    @pl.when(b < n_used)
    def _():
        gather_wait(cur)
        x = _unpack_pairs(_load_rows(xbuf.at[cur], 0, rows, k, k))
        copies = ([functools.partial(gather_row, tok_next_ref, nxt, r) for r in range(rows)]
                  + [functools.partial(scatter_row, slot_prev_ref, nxt, r) for r in range(rows)])
        n_pieces = de // piece + half // piece
        per = -(-len(copies) // n_pieces)

        def issue(i):
            for start in copies[i * per:(i + 1) * per]:
                start()

        acts = []
        for c in range(de // piece):
            issue(c)
            cols = slice(c * piece, (c + 1) * piece)
            hg = _dot(x, wg_ref[0, :, cols])
            acts.append(hg * jax.nn.sigmoid(hg) * _dot(x, wu_ref[0, :, cols]))
        act = jnp.concatenate(acts, axis=-1)
        for c in range(half // piece):
            issue(de // piece + c)
            y_pair = jnp.concatenate([_dot(act, wd_ref[0, :, c * piece:(c + 1) * piece]),
                                      _dot(act, wd_ref[0, :, half + c * piece:half + (c + 1) * piece])], axis=-1)
            words = _pack_pairs(y_pair)
            out = ybuf.at[cur]
            for j in range(piece // LANES):
                out[pl.ds(c * (piece // LANES) + j, rows, stride=k), :] = words[:, j * LANES:(j + 1) * LANES]

    @pl.when(b == n_used - 1)
    def _():
        for r in range(rows):
            scatter_row(slot_ref, cur, r)
        gather_wait(nxt)
        scatter_wait(nxt)
        scatter_wait(cur)


def _moe(h2, row_tok, row_slot, block_expert, n_used, w_g, w_u, w_d, n_slots):
    de, d = w_d.shape[-2:]
    k = d // (2 * LANES)
    n_blocks = row_tok.shape[0]
    smem = lambda off: pl.BlockSpec((1, 1, MOE_ROWS), lambda b, e, n: (jnp.minimum(b + off, n_blocks - 1), 0, 0),
                                    memory_space=pltpu.SMEM)
    spare = ((n_slots + jnp.arange(MOE_ROWS, dtype=jnp.int32)) * k).reshape(1, 1, MOE_ROWS)
    slot_ext = jnp.concatenate([spare, row_slot], axis=0)
    slot_spec = lambda off: pl.BlockSpec((1, 1, MOE_ROWS), lambda b, e, n: (b + off, 0, 0),
                                         memory_space=pltpu.SMEM)
    grid_spec = pltpu.PrefetchScalarGridSpec(
        num_scalar_prefetch=2,
        grid=(n_blocks,),
        in_specs=[smem(0), smem(1), slot_spec(0), slot_spec(1),
                  pl.BlockSpec(memory_space=pl.ANY),
                  pl.BlockSpec((1, d, de), lambda b, e, n: (e[b], 0, 0)),
                  pl.BlockSpec((1, d, de), lambda b, e, n: (e[b], 0, 0)),
                  pl.BlockSpec((1, de, d), lambda b, e, n: (e[b], 0, 0))],
        out_specs=pl.BlockSpec(memory_space=pl.ANY),
        scratch_shapes=[pltpu.VMEM((2, MOE_ROWS * k, LANES), jnp.uint32),
                        pltpu.VMEM((2, MOE_ROWS * k, LANES), jnp.uint32),
                        pltpu.SemaphoreType.DMA((2,)), pltpu.SemaphoreType.DMA((2,))],
    )
    return pl.pallas_call(
        _moe_kernel,
        grid_spec=grid_spec,
        out_shape=jax.ShapeDtypeStruct(((n_slots + MOE_ROWS) * k, LANES), jnp.uint32),
        compiler_params=_cparams("arbitrary"),
        name="moe",
    )(block_expert, n_used, row_tok, row_tok, slot_ext, slot_ext, h2, w_g, w_u, w_d)


def _dispatch_tables(route, counts, n_exp, t, k):
    n_slots = t * TOP_K
    n_blocks = n_slots // MOE_ROWS + n_exp
    slot_e = route[:, 0:TOP_K].astype(jnp.int32)
    rank = route[:, 4:4 + TOP_K].astype(jnp.int32)
    cnt = counts[0, :n_exp].astype(jnp.int32)
    padded = (cnt + MOE_ROWS - 1) // MOE_ROWS * MOE_ROWS
    pends = jnp.cumsum(padded)
    dest = ((pends - padded)[slot_e] + rank).reshape(-1)
    spare = n_slots + jnp.arange(n_blocks * MOE_ROWS, dtype=jnp.int32) % MOE_ROWS
    row_slot = spare.at[dest].set(jnp.arange(n_slots, dtype=jnp.int32))
    row_tok = jnp.where(row_slot < n_slots, row_slot // TOP_K, 0)
    block_start = jnp.arange(n_blocks, dtype=jnp.int32) * MOE_ROWS
    block_expert = jnp.minimum(jnp.searchsorted(pends, block_start, side='right'), n_exp - 1).astype(jnp.int32)
    n_used = (pends[-1:] // MOE_ROWS).astype(jnp.int32)
    shape3 = (n_blocks, 1, MOE_ROWS)
    return (row_tok * k).reshape(shape3), (row_slot * k).reshape(shape3), block_expert, n_used


def _final_kernel(y_ref, route_ref, x1_ref, gate_ref, g_ref, b_ref, o_ref, *, alpha):
    tm, d = x1_ref.shape
    k = d // (2 * LANES)
    route = route_ref[...]
    y0 = _unpack_pairs(_load_rows(y_ref, 0, tm, k, TOP_K * k))
    y1 = _unpack_pairs(_load_rows(y_ref, k, tm, k, TOP_K * k))
    ffn = route[:, 2:3] * y0 + route[:, 3:4] * y1
    o_ref[...] = _ln(alpha * x1_ref[...] + (1.0 + gate_ref[0]) * ffn) * g_ref[...] + b_ref[...]


def _final(y_slots, route, x1, gate2, g2, b2, seq, tm, alpha):
    t, d = x1.shape
    k = d // (2 * LANES)
    tiles_per_batch = seq // tm
    return pl.pallas_call(
        functools.partial(_final_kernel, alpha=alpha),
        grid=(t // tm,),
        in_specs=[pl.BlockSpec((tm * TOP_K * k, LANES), lambda i: (i, 0)),
                  pl.BlockSpec((tm, LANES), lambda i: (i, 0)),
                  pl.BlockSpec((tm, d), lambda i: (i, 0)),
                  pl.BlockSpec((1, 1, d), lambda i: (i // tiles_per_batch, 0, 0)),
                  pl.BlockSpec((1, d), lambda i: (0, 0)),
                  pl.BlockSpec((1, d), lambda i: (0, 0))],
        out_specs=pl.BlockSpec((tm, d), lambda i: (i, 0)),
        out_shape=jax.ShapeDtypeStruct((t, d), F32),
        compiler_params=_cparams("arbitrary"),
        name="final",
    )(y_slots, route, x1, gate2, g2.reshape(1, d), b2.reshape(1, d))


def _tiles(seq):
    return dict(proj=min(1024, seq), conv=min(512, seq), branch=min(256, seq), mix=min(256, seq),
                final=min(512, seq))


def kernel(x, c, w_ada, b_ada, w_in, b_in, s5_a_re, s5_a_im, s5_log_dt, s5_b_re, s5_b_im, s5_c_re, s5_c_im, s5_d, w_s5_gate, w_s5_up, conv_dw, conv_dw_b, conv_ln_g, conv_ln_b, w_conv_out, w_out, ln1_g, ln1_b, w_route_group, b_route_group, w_route_expert, b_route_expert, w_exp_gate, w_exp_up, w_exp_down, ln2_g, ln2_b):
    bsz, seq, d = x.shape
    t = bsz * seq
    depth = w_ada.shape[0]
    alpha = (2.0 * depth) ** 0.25
    s5_width = w_s5_gate.shape[1]
    conv_ch = conv_dw.shape[-1]
    n_grp = w_route_group.shape[-1]
    n_exp = w_route_expert.shape[-1]
    tl = _tiles(seq)
    tn = min(512, s5_width)
    x2 = x.reshape(t, d)
    for l in range(depth):
        mod = _ada(c, w_ada[l], b_ada[l])
        shift1, scale1, gate1, shift2, scale2, gate2 = [
            m.reshape(bsz, 1, d) for m in jnp.split(mod, 6, axis=-1)]

        u, v, gates = _proj(x2, shift1, scale1, w_in[l].astype(BF16), b_in[l], s5_width, conv_ch, seq,
                            tl["proj"], tn)
        tables = _s5_tables(s5_a_re[l], s5_a_im[l], s5_log_dt[l], s5_b_re[l], s5_b_im[l],
                            s5_c_re[l], s5_c_im[l], s5_d[l], S5_CHUNK)
        ys = _s5(u, tables, bsz, S5_CHUNK)
        cact = _conv(v.reshape(bsz, seq, conv_ch), conv_dw[l], conv_dw_b[l], conv_ln_g[l], conv_ln_b[l],
                     tl["conv"]).reshape(t, conv_ch)
        merged = _branch(ys, cact, gates, w_s5_gate[l].astype(BF16), w_s5_up[l].astype(BF16),
                         w_conv_out[l].astype(BF16), tl["branch"])

        w_r = jnp.zeros((d, LANES), F32).at[:, :n_exp].set(w_route_expert[l])
        w_r = w_r.at[:, n_exp:n_exp + n_grp].set(w_route_group[l])
        b_r = jnp.zeros((1, LANES), F32).at[0, :n_exp].set(b_route_expert[l])
        b_r = b_r.at[0, n_exp:n_exp + n_grp].set(b_route_group[l])
        x1, h2, route, counts = _mix(merged, x2, gate1, ln1_g[l], ln1_b[l], scale2, shift2,
                                     w_out[l].astype(BF16), w_r, b_r, seq, tl["mix"], alpha, n_exp, n_grp)
        row_tok, row_slot, block_expert, n_used = _dispatch_tables(route, counts, n_exp, t, d // (2 * LANES))
        y_slots = _moe(h2, row_tok, row_slot, block_expert, n_used, w_exp_gate[l], w_exp_up[l],
                       w_exp_down[l], t * TOP_K)
        x2 = _final(y_slots, route, x1, gate2, ln2_g[l], ln2_b[l], seq, tl["final"], alpha)
    return x2.reshape(bsz, seq, d)
```

```python
import functools

import jax
import jax.numpy as jnp
from jax import lax
from jax.experimental import pallas as pl
from jax.experimental.pallas import tpu as pltpu

F32 = jnp.float32
BF16 = jnp.bfloat16
LN_EPS = 1e-5
TOP_K = 2
LANES = 128
S5_CHUNK = 16
CONV_HALO = 32
MOE_ROWS = 256
MOE_DMA_GROUP = 32
VMEM_LIMIT = 56 * 1024 * 1024


def _cparams(*sem):
    return pltpu.CompilerParams(dimension_semantics=sem, vmem_limit_bytes=VMEM_LIMIT)


def _ln(x):
    mu = jnp.mean(x, axis=-1, keepdims=True)
    xc = x - mu
    var = jnp.mean(xc * xc, axis=-1, keepdims=True)
    return xc * lax.rsqrt(var + LN_EPS)


def _dot(a, b):
    return jnp.dot(a, b, preferred_element_type=F32)


def _sigmoid(x):
    return 0.5 * jnp.tanh(0.5 * x) + 0.5


def _pack_pairs(a):
    w = a.shape[1] // 2
    hi = lax.bitcast_convert_type(a[:, :w].astype(BF16).astype(F32), jnp.uint32)
    lo = lax.bitcast_convert_type(a[:, w:].astype(BF16).astype(F32), jnp.uint32)
    return hi | (lo >> 16)


def _unpack_pairs(words):
    hi = lax.bitcast_convert_type(words & jnp.uint32(0xFFFF0000), F32)
    lo = lax.bitcast_convert_type(words << 16, F32)
    return jnp.concatenate([hi, lo], axis=-1)


def _store_rows(ref, first, n, k, packed):
    for j in range(k):
        ref[pl.ds(first + j, n, stride=k), :] = packed[:, j * LANES:(j + 1) * LANES]


def _load_rows(ref, first, n, k, stride):
    return jnp.concatenate([ref[pl.ds(first + j, n, stride=stride), :] for j in range(k)], axis=-1)


def _ada_kernel(c_ref, w_ref, b_ref, o_ref):
    c = c_ref[...]
    o_ref[...] = _dot(c * jax.nn.sigmoid(c), w_ref[...]) + b_ref[...]


def _ada(c, w, b, tn=1024):
    bsz, d = c.shape
    n = w.shape[1]
    rows = 8
    cp = jnp.zeros((rows, d), F32).at[:bsz].set(c)
    out = pl.pallas_call(
        _ada_kernel,
        grid=(n // tn,),
        in_specs=[pl.BlockSpec((rows, d), lambda j: (0, 0)),
                  pl.BlockSpec((d, tn), lambda j: (0, j)),
                  pl.BlockSpec((1, tn), lambda j: (0, j))],
        out_specs=pl.BlockSpec((rows, tn), lambda j: (0, j)),
        out_shape=jax.ShapeDtypeStruct((rows, n), F32),
        compiler_params=_cparams("arbitrary"),
        name="ada",
    )(cp, w, b.reshape(1, n))
    return out[:bsz]


def _proj_kernel(x_ref, shift_ref, scale_ref, w_ref, b_ref, u_ref, v_ref, g_ref, h_scr, a_scr,
                 *, n_u, n_a):
    j = pl.program_id(1)

    @pl.when(j == 0)
    def _():
        h = _ln(x_ref[...]) * (1.0 + scale_ref[0]) + shift_ref[0]
        h_scr[...] = h.astype(BF16)

    p = _dot(h_scr[...], w_ref[...]) + b_ref[...]

    @pl.when(j < n_u)
    def _():
        for q in range(u_ref.shape[0]):
            u_ref[q] = p[:, q * LANES:(q + 1) * LANES]

    @pl.when((j >= n_u) & (j < n_u + n_a))
    def _():
        a_scr[j - n_u] = p

    @pl.when((j >= n_u + n_a) & (j < n_u + 2 * n_a))
    def _():
        v_ref[...] = a_scr[j - n_u - n_a] * _sigmoid(p)

    @pl.when(j >= n_u + 2 * n_a)
    def _():
        g_ref[...] = _sigmoid(p).astype(BF16)


def _proj(x2, shift, scale, w_bf, b, s5_width, conv_ch, seq, tm, tn):
    t, d = x2.shape
    n = w_bf.shape[1]
    n_u, n_a = s5_width // tn, conv_ch // tn
    assert n_u == 1, "the S5 input must be one column tile"
    n_lt = s5_width // LANES
    n_g = (n - s5_width - 2 * conv_ch) // tn
    tiles_per_batch = seq // tm
    g0 = n_u + 2 * n_a
    kern = functools.partial(_proj_kernel, n_u=n_u, n_a=n_a)
    return pl.pallas_call(
        kern,
        grid=(t // tm, n // tn),
        in_specs=[pl.BlockSpec((tm, d), lambda i, j: (i, 0)),
                  pl.BlockSpec((1, 1, d), lambda i, j: (i // tiles_per_batch, 0, 0)),
                  pl.BlockSpec((1, 1, d), lambda i, j: (i // tiles_per_batch, 0, 0)),
                  pl.BlockSpec((d, tn), lambda i, j: (0, j)),
                  pl.BlockSpec((1, tn), lambda i, j: (0, j))],
        out_specs=[pl.BlockSpec((n_lt, tm, LANES), lambda i, j: (0, i, 0)),
                   pl.BlockSpec((tm, tn), lambda i, j: (i, jnp.clip(j - n_u - n_a, 0, n_a - 1))),
                   pl.BlockSpec((tm, tn), lambda i, j: (i, jnp.clip(j - g0, 0, n_g - 1)))],
        out_shape=[jax.ShapeDtypeStruct((n_lt, t, LANES), F32),
                   jax.ShapeDtypeStruct((t, conv_ch), F32),
                   jax.ShapeDtypeStruct((t, n_g * tn), BF16)],
        scratch_shapes=[pltpu.VMEM((tm, d), BF16), pltpu.VMEM((n_a, tm, tn), F32)],
        compiler_params=_cparams("arbitrary", "arbitrary"),
        name="proj",
    )(x2, shift, scale, w_bf, b.reshape(1, n))


def _s5_tables(a_re, a_im, log_dt, b_re, b_im, c_re, c_im, d_skip, chunk):
    hp = lax.Precision.HIGHEST
    g, p = a_re.shape
    hw = b_re.shape[-1]
    w = chunk * hw
    dt = jnp.exp(log_dt.astype(F32))[:, None, None]
    lr, li = a_re.astype(F32)[:, :, None], a_im.astype(F32)[:, :, None]

    def power(k):
        mag = jnp.exp(k * lr * dt)
        return mag * jnp.cos(k * li * dt), mag * jnp.sin(k * li * dt)

    a1_re, a1_im = power(1.0)
    den = lr * lr + li * li
    nr, ni = a1_re - 1.0, a1_im
    z_re = (nr * lr + ni * li) / den
    z_im = (ni * lr - nr * li) / den
    br, bi = b_re.astype(F32), b_im.astype(F32)
    bb_re = z_re * br - z_im * bi
    bb_im = z_re * bi + z_im * br
    lag = (jnp.arange(w + hw) // hw).astype(F32)[None, None, :]
    pw_re, pw_im = power(lag)
    ct_re = jnp.tile(c_re.astype(F32).transpose(0, 2, 1), (1, 1, chunk + 1))
    ct_im = jnp.tile(c_im.astype(F32).transpose(0, 2, 1), (1, 1, chunk + 1))
    ca_re = ct_re * pw_re - ct_im * pw_im
    ca_im = ct_re * pw_im + ct_im * pw_re
    kt = (jnp.einsum('gpj,gpl->gjl', bb_re, ca_re[:, :, :w], precision=hp)
          - jnp.einsum('gpj,gpl->gjl', bb_im, ca_im[:, :, :w], precision=hp))
    m = jnp.stack([jnp.pad(kt[:, :, :w - s * hw], ((0, 0), (0, 0), (s * hw, 0))) for s in range(chunk)],
                  axis=1).reshape(g, w, w)
    rev = (chunk - 1 - jnp.arange(w) // hw).astype(F32)[None, None, :]
    rv_re, rv_im = power(rev)
    bt_re, bt_im = jnp.tile(bb_re, (1, 1, chunk)), jnp.tile(bb_im, (1, 1, chunk))
    pt_re = rv_re * bt_re - rv_im * bt_im
    pt_im = rv_re * bt_im + rv_im * bt_re
    qo_re, qo_im = ca_re[:, :, hw:], -ca_im[:, :, hw:]
    al_re, al_im = power(float(chunk))
    al_re, al_im = al_re.reshape(g, 1, p), al_im.reshape(g, 1, p)
    d_t = jnp.tile(d_skip.astype(F32)[:, None, :], (1, 1, chunk))
    return m, pt_re, pt_im, qo_re, qo_im, al_re, al_im, d_t


def _s5_kernel(u_ref, m_ref, pre_ref, pim_ref, qre_ref, qim_ref, are_ref, aim_ref, d_ref, y_ref,
               zre, zim, sre, sim, *, bsz):
    u = u_ref[0]
    rows = u.shape[0]
    nc = rows // bsz
    y = _dot(u, m_ref[0]) + d_ref[0] * u
    nt = (((1,), (1,)), ((), ()))
    zre[...] = lax.dot_general(u, pre_ref[0], nt, preferred_element_type=F32)
    zim[...] = lax.dot_general(u, pim_ref[0], nt, preferred_element_type=F32)
    ar, ai = are_ref[0], aim_ref[0]
    npair = zre.shape[1]

    def step(c, carry):
        nxt = []
        for b in range(bsz):
            sr, si = carry[2 * b], carry[2 * b + 1]
            row = pl.ds(b * nc + c, 1)
            sre[row, :] = sr
            sim[row, :] = si
            nxt.append(ar * sr - ai * si + zre[row, :])
            nxt.append(ar * si + ai * sr + zim[row, :])
        return tuple(nxt)

    zero = jnp.zeros((1, npair), F32)
    lax.fori_loop(0, nc, step, tuple(zero for _ in range(2 * bsz)))
    y = y + _dot(sre[...], qre_ref[0]) + _dot(sim[...], qim_ref[0])
    y_ref[0] = jax.nn.gelu(y, approximate=True)


def _pack_kernel(u_ref, o_ref, *, chunk):
    g, cb, _ = o_ref.shape
    n_lt = u_ref.shape[0]
    per = g // n_lt
    hw = LANES // per
    for s in range(chunk):
        for q in range(n_lt):
            rows = u_ref[q, pl.ds(s, cb, stride=chunk), :]
            for i in range(per):
                o_ref[q * per + i, :, s * hw:(s + 1) * hw] = rows[:, i * hw:(i + 1) * hw]


def _unpack_kernel(y_ref, o_ref, tmp, *, chunk):
    g, cb, _ = y_ref.shape
    n_lt = o_ref.shape[0]
    per = g // n_lt
    hw = LANES // per
    for s in range(chunk):
        for q in range(n_lt):
            for i in range(per):
                tmp[:, i * hw:(i + 1) * hw] = y_ref[q * per + i, :, s * hw:(s + 1) * hw]
            o_ref[q, pl.ds(s, cb, stride=chunk), :] = tmp[...]


def _s5(u, tables, bsz, chunk):
    m, p_re, p_im, q_re, q_im, a_re, a_im, d_t = tables
    n_lt, t, _ = u.shape
    g, cw, _ = m.shape
    npair = p_re.shape[1]
    rows = t // chunk
    cb = min(64, rows)
    by_chunk = pl.BlockSpec((n_lt, cb * chunk, LANES), lambda i: (0, i, 0))
    by_group = pl.BlockSpec((g, cb, cw), lambda i: (0, i, 0))
    ut = pl.pallas_call(
        functools.partial(_pack_kernel, chunk=chunk), grid=(rows // cb,), in_specs=[by_chunk],
        out_specs=by_group, out_shape=jax.ShapeDtypeStruct((g, rows, cw), F32),
        compiler_params=_cparams("arbitrary"), name="s5_pack",
    )(u)
    gspec = lambda *shape: pl.BlockSpec((1,) + shape, lambda i: (i, 0, 0))
    yt = pl.pallas_call(
        functools.partial(_s5_kernel, bsz=bsz),
        grid=(g,),
        in_specs=[gspec(rows, cw), gspec(cw, cw), gspec(npair, cw), gspec(npair, cw),
                  gspec(npair, cw), gspec(npair, cw), gspec(1, npair), gspec(1, npair), gspec(1, cw)],
        out_specs=gspec(rows, cw),
        out_shape=jax.ShapeDtypeStruct((g, rows, cw), F32),
        scratch_shapes=[pltpu.VMEM((rows, npair), F32) for _ in range(4)],
        compiler_params=_cparams("arbitrary"),
        name="s5",
    )(ut, m, p_re, p_im, q_re, q_im, a_re, a_im, d_t)
    return pl.pallas_call(
        functools.partial(_unpack_kernel, chunk=chunk), grid=(rows // cb,), in_specs=[by_group],
        out_specs=by_chunk, out_shape=jax.ShapeDtypeStruct((n_lt, t, LANES), F32),
        scratch_shapes=[pltpu.VMEM((cb, LANES), F32)],
        compiler_params=_cparams("arbitrary"), name="s5_unpack",
    )(yt)


def _conv_window(win, w_ref, b_ref, g_ref, beta_ref, ts, span):
    sub = 8
    base = CONV_HALO - (span - 1)
    acc = None
    for q in range(sub):
        part = None
        for k in range(span):
            if (base + k) % sub == q:
                lo = base + k - q
                term = win[lo:lo + ts + sub, :] * w_ref[k:k + 1, :]
                part = term if part is None else part + term
        if part is not None:
            part = part[q:q + ts, :]
            acc = part if acc is None else acc + part
    y = _ln(acc + b_ref[...]) * g_ref[...] + beta_ref[...]
    return y * jax.nn.sigmoid(y)


def _branch_kernel(ys_ref, v0_ref, vn_ref, gs_ref, wgate_ref, wup_ref, wco_ref, wdw_ref, bdw_ref, lng_ref,
                   lnb_ref, o_ref, win, cact, *, span, tiles_per_batch):
    i = pl.program_id(0)
    tm = o_ref.shape[0]
    zeros = functools.partial(jnp.zeros, dtype=F32)

    def conv_tile(v_ref, slot):
        win[CONV_HALO:CONV_HALO + tm, :] = v_ref[...]
        act = _conv_window(win, wdw_ref, bdw_ref, lng_ref, lnb_ref, tm, span)
        cact[slot] = act.astype(BF16)
        win[0:CONV_HALO, :] = win[tm:tm + CONV_HALO, :]

    @pl.when(i == 0)
    def _():
        win[0:CONV_HALO, :] = zeros((CONV_HALO, win.shape[1]))
        win[CONV_HALO + tm:, :] = zeros((win.shape[0] - CONV_HALO - tm, win.shape[1]))
        conv_tile(v0_ref, 0)

    @pl.when((i + 1) % tiles_per_batch == 0)
    def _():
        win[0:CONV_HALO, :] = zeros((CONV_HALO, win.shape[1]))

    conv_tile(vn_ref, (i + 1) % 2)
    ys = jnp.concatenate([ys_ref[q] for q in range(ys_ref.shape[0])], axis=-1)
    glu = ys * jax.nn.sigmoid(_dot(ys.astype(BF16), wgate_ref[...]))
    y_s5 = _dot(glu.astype(BF16), wup_ref[...])
    y_conv = _dot(cact[i % 2], wco_ref[...])
    d = y_s5.shape[1]
    merged = gs_ref[:, :d].astype(F32) * y_s5 + gs_ref[:, d:].astype(F32) * y_conv
    o_ref[...] = merged.astype(BF16)


def _branch(ys, v, gates, w_gate, w_up, w_co, w_dw, b_dw, ln_g, ln_b, seq, tm):
    n_lt, t, _ = ys.shape
    cc = v.shape[1]
    d = w_up.shape[1]
    span = w_dw.shape[0]
    n_tiles = t // tm
    full = lambda a: pl.BlockSpec(a.shape, lambda i: (0, 0))
    row = lambda a: a.reshape(1, cc)
    w_dw, b_dw, ln_g, ln_b = w_dw.reshape(span, cc), row(b_dw), row(ln_g), row(ln_b)
    kern = functools.partial(_branch_kernel, span=span, tiles_per_batch=seq // tm)
    return pl.pallas_call(
        kern,
        grid=(n_tiles,),
        in_specs=[pl.BlockSpec((n_lt, tm, LANES), lambda i: (0, i, 0)),
                  pl.BlockSpec((tm, cc), lambda i: (0, 0)),
                  pl.BlockSpec((tm, cc), lambda i: (jnp.minimum(i + 1, n_tiles - 1), 0)),
                  pl.BlockSpec((tm, 2 * d), lambda i: (i, 0)),
                  full(w_gate), full(w_up), full(w_co), full(w_dw), full(b_dw), full(ln_g), full(ln_b)],
        out_specs=pl.BlockSpec((tm, d), lambda i: (i, 0)),
        out_shape=jax.ShapeDtypeStruct((t, d), BF16),
        scratch_shapes=[pltpu.VMEM((CONV_HALO + tm + 8, cc), F32), pltpu.VMEM((2, tm, cc), BF16)],
        compiler_params=_cparams("arbitrary"),
        name="branch",
    )(ys, v, v, gates, w_gate, w_up, w_co, w_dw, b_dw, ln_g, ln_b)


def _mix_kernel(m_ref, x_ref, gate_ref, g1_ref, b1_ref, scale2_ref, shift2_ref, wout_ref, wr_ref, br_ref,
                x1_ref, h2_ref, route_ref, routet_ref, cnt_ref, run, *, alpha, n_exp, n_grp):
    @pl.when(pl.program_id(0) == 0)
    def _():
        run[...] = jnp.zeros_like(run)

    mix = _dot(m_ref[...], wout_ref[...])
    x1 = _ln(alpha * x_ref[...] + (1.0 + gate_ref[0]) * mix) * g1_ref[...] + b1_ref[...]
    x1_ref[...] = x1
    h2 = _ln(x1) * (1.0 + scale2_ref[0]) + shift2_ref[0]
    _store_rows(h2_ref, 0, h2.shape[0], h2.shape[1] // (2 * LANES), _pack_pairs(h2))

    logits = _dot(h2, wr_ref[...]) + br_ref[...]
    tm = logits.shape[0]
    per = n_exp // n_grp
    lane = lax.broadcasted_iota(jnp.int32, logits.shape, 1)
    big = jnp.int32(LANES)
    neg = jnp.float32(-jnp.inf)

    def first_max(val):
        top = jnp.max(val, axis=-1, keepdims=True)
        return top, jnp.min(jnp.where(val == top, lane, big), axis=-1, keepdims=True)

    gmask = (lane >= n_exp) & (lane < n_exp + n_grp)
    gtop, glane = first_max(jnp.where(gmask, logits, neg))
    grp = glane - n_exp
    grp_w = 1.0 / jnp.sum(jnp.where(gmask, jnp.exp(logits - gtop), 0.0), axis=-1, keepdims=True)
    emask = (lane >= grp * per) & (lane < grp * per + per)
    el = jnp.where(emask, logits, neg)
    t1, e1 = first_max(el)
    t2, e2 = first_max(jnp.where(lane == e1, neg, el))
    ex = jnp.exp(t2 - t1)
    w1 = grp_w / (1.0 + ex)
    w2 = grp_w * ex / (1.0 + ex)

    oh1 = (lane == e1).astype(F32)
    oh2 = (lane == e2).astype(F32)
    both = oh1 + oh2
    r_i = lax.broadcasted_iota(jnp.int32, (tm, tm), 0)
    c_i = lax.broadcasted_iota(jnp.int32, (tm, tm), 1)
    before = _dot((r_i > c_i).astype(F32), both) + run[...]
    rank1 = jnp.sum(before * oh1, axis=-1, keepdims=True)
    rank2 = jnp.sum(before * oh2, axis=-1, keepdims=True)
    run[...] = run[...] + jnp.sum(both, axis=0, keepdims=True)
    cnt_ref[...] = run[...]

    route = jnp.where(lane == 0, e1.astype(F32), 0.0)
    route = jnp.where(lane == 1, e2.astype(F32), route)
    route = jnp.where(lane == 2, w1, route)
    route = jnp.where(lane == 3, w2, route)
    route = jnp.where(lane == 4, rank1, route)
    route = jnp.where(lane == 5, rank2, route)
    route_ref[...] = route
    routet_ref[...] = route.T[0:routet_ref.shape[0], :]


def _mix(merged, x2, gate1, g1, b1, scale2, shift2, w_out, w_r, b_r, seq, tm, alpha, n_exp, n_grp):
    t, d = x2.shape
    k = d // (2 * LANES)
    tiles_per_batch = seq // tm
    tile = pl.BlockSpec((tm, d), lambda i: (i, 0))
    per_batch = pl.BlockSpec((1, 1, d), lambda i: (i // tiles_per_batch, 0, 0))
    const = lambda a: pl.BlockSpec(a.shape, lambda i: (0, 0))
    g1, b1 = g1.reshape(1, d), b1.reshape(1, d)
    kern = functools.partial(_mix_kernel, alpha=alpha, n_exp=n_exp, n_grp=n_grp)
    return pl.pallas_call(
        kern,
        grid=(t // tm,),
        in_specs=[tile, tile, per_batch, const(g1), const(b1), per_batch, per_batch,
                  const(w_out), const(w_r), const(b_r)],
        out_specs=[tile, pl.BlockSpec((tm * k, LANES), lambda i: (i, 0)),
                   pl.BlockSpec((tm, LANES), lambda i: (i, 0)),
                   pl.BlockSpec((8, tm), lambda i: (0, i)),
                   pl.BlockSpec((1, LANES), lambda i: (0, 0))],
        out_shape=[jax.ShapeDtypeStruct((t, d), F32), jax.ShapeDtypeStruct((t * k, LANES), jnp.uint32),
                   jax.ShapeDtypeStruct((t, LANES), F32), jax.ShapeDtypeStruct((8, t), F32),
                   jax.ShapeDtypeStruct((1, LANES), F32)],
        scratch_shapes=[pltpu.VMEM((1, LANES), F32)],
        compiler_params=_cparams("arbitrary"),
        name="mix_route",
    )(merged, x2, gate1, g1, b1, scale2, shift2, w_out, w_r, b_r)


def _moe_kernel(bexp_ref, nused_ref, nvalid_ref, tok_ref, tok_next_ref, slot_ref, h_hbm, wg_ref, wu_ref, wd_ref,
                y_hbm, xbuf, ybuf, gsem, ssem):
    del bexp_ref
    rows = tok_ref.shape[2]
    k = xbuf.shape[1] // rows
    group = MOE_DMA_GROUP
    b = pl.program_id(0)
    last = pl.num_programs(0) - 1
    n_used = nused_ref[0]
    cur, nxt = b % 2, (b + 1) % 2
    nv_cur = nvalid_ref[b]
    nv_next = nvalid_ref[jnp.minimum(b + 1, last)]

    def per_group(n_valid, fn):
        for g0 in range(0, rows, group):
            pl.when(g0 < n_valid)(functools.partial(fn, g0))

    def gather_start(idx_ref, buf, n_valid):
        def start(g0):
            for r in range(g0, g0 + group):
                src = h_hbm.at[pl.ds(pl.multiple_of(idx_ref[0, 0, r], k), k)]
                pltpu.make_async_copy(src, xbuf.at[buf, pl.ds(r * k, k)], gsem.at[buf]).start()
        per_group(n_valid, start)

    def gather_wait(buf, n_valid):
        def wait(g0):
            pltpu.make_async_copy(h_hbm.at[pl.ds(0, group * k)], xbuf.at[buf, pl.ds(g0 * k, group * k)],
                                  gsem.at[buf]).wait()
        per_group(n_valid, wait)

    def scatter_start(buf, n_valid):
        def start(g0):
            for r in range(g0, g0 + group):
                dst = y_hbm.at[pl.ds(pl.multiple_of(slot_ref[0, 0, r], k), k)]
                pltpu.make_async_copy(ybuf.at[buf, pl.ds(r * k, k)], dst, ssem.at[buf]).start()
        per_group(n_valid, start)

    def scatter_wait(buf, n_valid):
        def wait(g0):
            pltpu.make_async_copy(ybuf.at[buf, pl.ds(g0 * k, group * k)], y_hbm.at[pl.ds(0, group * k)],
                                  ssem.at[buf]).wait()
        per_group(n_valid, wait)

    @pl.when(b == 0)
    def _():
        xbuf[...] = jnp.zeros(xbuf.shape, xbuf.dtype)
        ybuf[1] = jnp.zeros(ybuf.shape[1:], ybuf.dtype)
        spare = pltpu.make_async_copy(ybuf.at[1], y_hbm.at[pl.ds(y_hbm.shape[0] - rows * k, rows * k)], ssem.at[1])
        spare.start()
        spare.wait()
        gather_start(tok_ref, 0, nv_cur)

    @pl.when((b >= 2) & (b < n_used))
    def _():
        scatter_wait(cur, nvalid_ref[jnp.maximum(b - 2, 0)])

    @pl.when(b < n_used)
    def _():
        gather_start(tok_next_ref, nxt, nv_next)
        gather_wait(cur, nv_cur)
        x = _unpack_pairs(_load_rows(xbuf.at[cur], 0, rows, k, k))
        hg = _dot(x, wg_ref[0])
        act = hg * jax.nn.sigmoid(hg) * _dot(x, wu_ref[0])
        _store_rows(ybuf.at[cur], 0, rows, k, _pack_pairs(_dot(act, wd_ref[0])))
        scatter_start(cur, nv_cur)

    @pl.when(b == n_used - 1)
    def _():
        gather_wait(nxt, nv_next)
        scatter_wait(cur, nv_cur)

        @pl.when(b >= 1)
        def _():
            scatter_wait(nxt, nvalid_ref[jnp.maximum(b - 1, 0)])


def _moe(h2, row_tok, row_slot, block_expert, n_used, n_valid, w_g, w_u, w_d, n_slots):
    de, d = w_d.shape[-2:]
    k = d // (2 * LANES)
    n_blocks = row_tok.shape[0]
    smem = lambda off: pl.BlockSpec((1, 1, MOE_ROWS),
                                    lambda b, e, n, v: (jnp.minimum(b + off, n_blocks - 1), 0, 0),
                                    memory_space=pltpu.SMEM)
    grid_spec = pltpu.PrefetchScalarGridSpec(
        num_scalar_prefetch=3,
        grid=(n_blocks,),
        in_specs=[smem(0), smem(1), smem(0),
                  pl.BlockSpec(memory_space=pl.ANY),
                  pl.BlockSpec((1, d, de), lambda b, e, n, v: (e[b], 0, 0)),
                  pl.BlockSpec((1, d, de), lambda b, e, n, v: (e[b], 0, 0)),
                  pl.BlockSpec((1, de, d), lambda b, e, n, v: (e[b], 0, 0))],
        out_specs=pl.BlockSpec(memory_space=pl.ANY),
        scratch_shapes=[pltpu.VMEM((2, MOE_ROWS * k, LANES), jnp.uint32),
                        pltpu.VMEM((2, MOE_ROWS * k, LANES), jnp.uint32),
                        pltpu.SemaphoreType.DMA((2,)), pltpu.SemaphoreType.DMA((2,))],
    )
    return pl.pallas_call(
        _moe_kernel,
        grid_spec=grid_spec,
        out_shape=jax.ShapeDtypeStruct(((n_slots + MOE_ROWS) * k, LANES), jnp.uint32),
        compiler_params=_cparams("arbitrary"),
        name="moe",
    )(block_expert, n_used, n_valid, row_tok, row_tok, row_slot, h2, w_g, w_u, w_d)


def _dispatch_tables(route_t, counts, n_exp, k):
    t = route_t.shape[1]
    n_slots = t * TOP_K
    n_blocks = n_slots // MOE_ROWS + n_exp
    slot_e = route_t[0:TOP_K].astype(jnp.int32)
    rank = route_t[4:4 + TOP_K].astype(jnp.int32)
    cnt = counts[0, :n_exp].astype(jnp.int32)
    padded = (cnt + MOE_ROWS - 1) // MOE_ROWS * MOE_ROWS
    pends = jnp.cumsum(padded)
    pstarts = pends - padded
    dest = (pstarts[slot_e] + rank).reshape(-1)
    slot_id = (TOP_K * jnp.arange(t, dtype=jnp.int32)[None, :] + jnp.arange(TOP_K, dtype=jnp.int32)[:, None])
    spare = n_slots + jnp.arange(n_blocks * MOE_ROWS, dtype=jnp.int32) % MOE_ROWS
    row_slot = spare.at[dest].set(slot_id.reshape(-1))
    row_tok = jnp.where(row_slot < n_slots, row_slot // TOP_K, 0)
    block_start = jnp.arange(n_blocks, dtype=jnp.int32) * MOE_ROWS
    block_expert = jnp.minimum(jnp.searchsorted(pends, block_start, side='right'), n_exp - 1).astype(jnp.int32)
    n_valid = jnp.clip((pstarts + cnt)[block_expert] - block_start, 0, MOE_ROWS).astype(jnp.int32)
    n_used = (pends[-1:] // MOE_ROWS).astype(jnp.int32)
    shape3 = (n_blocks, 1, MOE_ROWS)
    return (row_tok * k).reshape(shape3), (row_slot * k).reshape(shape3), block_expert, n_used, n_valid


def _final_kernel(y_ref, route_ref, x1_ref, gate_ref, g_ref, b_ref, o_ref, *, alpha):
    tm, d = x1_ref.shape
    k = d // (2 * LANES)
    route = route_ref[...]
    y0 = _unpack_pairs(_load_rows(y_ref, 0, tm, k, TOP_K * k))
    y1 = _unpack_pairs(_load_rows(y_ref, k, tm, k, TOP_K * k))
    ffn = route[:, 2:3] * y0 + route[:, 3:4] * y1
    o_ref[...] = _ln(alpha * x1_ref[...] + (1.0 + gate_ref[0]) * ffn) * g_ref[...] + b_ref[...]


def _final(y_slots, route, x1, gate2, g2, b2, seq, tm, alpha):
    t, d = x1.shape
    k = d // (2 * LANES)
    tiles_per_batch = seq // tm
    return pl.pallas_call(
        functools.partial(_final_kernel, alpha=alpha),
        grid=(t // tm,),
        in_specs=[pl.BlockSpec((tm * TOP_K * k, LANES), lambda i: (i, 0)),
                  pl.BlockSpec((tm, LANES), lambda i: (i, 0)),
                  pl.BlockSpec((tm, d), lambda i: (i, 0)),
                  pl.BlockSpec((1, 1, d), lambda i: (i // tiles_per_batch, 0, 0)),
                  pl.BlockSpec((1, d), lambda i: (0, 0)),
                  pl.BlockSpec((1, d), lambda i: (0, 0))],
        out_specs=pl.BlockSpec((tm, d), lambda i: (i, 0)),
        out_shape=jax.ShapeDtypeStruct((t, d), F32),
        compiler_params=_cparams("arbitrary"),
        name="final",
    )(y_slots, route, x1, gate2, g2.reshape(1, d), b2.reshape(1, d))


def _tiles(seq):
    return dict(proj=min(1024, seq), branch=min(256, seq), mix=min(256, seq), final=min(512, seq))


def kernel(x, c, w_ada, b_ada, w_in, b_in, s5_a_re, s5_a_im, s5_log_dt, s5_b_re, s5_b_im, s5_c_re, s5_c_im, s5_d, w_s5_gate, w_s5_up, conv_dw, conv_dw_b, conv_ln_g, conv_ln_b, w_conv_out, w_out, ln1_g, ln1_b, w_route_group, b_route_group, w_route_expert, b_route_expert, w_exp_gate, w_exp_up, w_exp_down, ln2_g, ln2_b):
    bsz, seq, d = x.shape
    t = bsz * seq
    depth = w_ada.shape[0]
    alpha = (2.0 * depth) ** 0.25
    s5_width = w_s5_gate.shape[1]
    conv_ch = conv_dw.shape[-1]
    n_grp = w_route_group.shape[-1]
    n_exp = w_route_expert.shape[-1]
    tl = _tiles(seq)
    tn = min(512, s5_width)
    x2 = x.reshape(t, d)
    for l in range(depth):
        mod = _ada(c, w_ada[l], b_ada[l])
        shift1, scale1, gate1, shift2, scale2, gate2 = [
            m.reshape(bsz, 1, d) for m in jnp.split(mod, 6, axis=-1)]

        u, v, gates = _proj(x2, shift1, scale1, w_in[l].astype(BF16), b_in[l], s5_width, conv_ch, seq,
                            tl["proj"], tn)
        tables = _s5_tables(s5_a_re[l], s5_a_im[l], s5_log_dt[l], s5_b_re[l], s5_b_im[l],
                            s5_c_re[l], s5_c_im[l], s5_d[l], S5_CHUNK)
        ys = _s5(u, tables, bsz, S5_CHUNK)
        merged = _branch(ys, v, gates, w_s5_gate[l].astype(BF16), w_s5_up[l].astype(BF16),
                         w_conv_out[l].astype(BF16), conv_dw[l], conv_dw_b[l], conv_ln_g[l], conv_ln_b[l],
                         seq, tl["branch"])

        w_r = jnp.zeros((d, LANES), F32).at[:, :n_exp].set(w_route_expert[l])
        w_r = w_r.at[:, n_exp:n_exp + n_grp].set(w_route_group[l])
        b_r = jnp.zeros((1, LANES), F32).at[0, :n_exp].set(b_route_expert[l])
        b_r = b_r.at[0, n_exp:n_exp + n_grp].set(b_route_group[l])
        x1, h2, route, route_t, counts = _mix(merged, x2, gate1, ln1_g[l], ln1_b[l], scale2, shift2,
                                              w_out[l].astype(BF16), w_r, b_r, seq, tl["mix"], alpha, n_exp, n_grp)
        row_tok, row_slot, block_expert, n_used, n_valid = _dispatch_tables(route_t, counts, n_exp,
                                                                            d // (2 * LANES))
        y_slots = _moe(h2, row_tok, row_slot, block_expert, n_used, n_valid, w_exp_gate[l], w_exp_up[l],
                       w_exp_down[l], t * TOP_K)
        x2 = _final(y_slots, route, x1, gate2, ln2_g[l], ln2_b[l], seq, tl["final"], alpha)
    return x2.reshape(bsz, seq, d)
```

```python
import functools

import jax
import jax.numpy as jnp
from jax import lax
from jax.experimental import pallas as pl
from jax.experimental.pallas import tpu as pltpu

F32 = jnp.float32
BF16 = jnp.bfloat16
LN_EPS = 1e-5
TOP_K = 2
LANES = 128
S5_CHUNK = 16
CONV_HALO = 32
MOE_ROWS = 256
MOE_DMA_GROUP = 32
VMEM_LIMIT = 56 * 1024 * 1024


def _cparams(*sem):
    return pltpu.CompilerParams(dimension_semantics=sem, vmem_limit_bytes=VMEM_LIMIT)


def _ln(x):
    mu = jnp.mean(x, axis=-1, keepdims=True)
    xc = x - mu
    var = jnp.mean(xc * xc, axis=-1, keepdims=True)
    return xc * lax.rsqrt(var + LN_EPS)


def _dot(a, b):
    return jnp.dot(a, b, preferred_element_type=F32)


def _sigmoid(x):
    return 0.5 * jnp.tanh(0.5 * x) + 0.5


def _pack_pairs(a):
    w = a.shape[1] // 2
    hi = lax.bitcast_convert_type(a[:, :w].astype(BF16).astype(F32), jnp.uint32)
    lo = lax.bitcast_convert_type(a[:, w:].astype(BF16).astype(F32), jnp.uint32)
    return hi | (lo >> 16)


def _unpack_pairs(words):
    hi = lax.bitcast_convert_type(words & jnp.uint32(0xFFFF0000), F32)
    lo = lax.bitcast_convert_type(words << 16, F32)
    return jnp.concatenate([hi, lo], axis=-1)


def _store_rows(ref, first, n, k, packed):
    for j in range(k):
        ref[pl.ds(first + j, n, stride=k), :] = packed[:, j * LANES:(j + 1) * LANES]


def _load_rows(ref, first, n, k, stride):
    return jnp.concatenate([ref[pl.ds(first + j, n, stride=stride), :] for j in range(k)], axis=-1)


def _ada_kernel(c_ref, w_ref, b_ref, o_ref):
    c = c_ref[...]
    o_ref[...] = _dot(c * jax.nn.sigmoid(c), w_ref[...]) + b_ref[...]


def _ada(c, w, b, tn=1024):
    bsz, d = c.shape
    n = w.shape[1]
    rows = 8
    cp = jnp.zeros((rows, d), F32).at[:bsz].set(c)
    out = pl.pallas_call(
        _ada_kernel,
        grid=(n // tn,),
        in_specs=[pl.BlockSpec((rows, d), lambda j: (0, 0)),
                  pl.BlockSpec((d, tn), lambda j: (0, j)),
                  pl.BlockSpec((1, tn), lambda j: (0, j))],
        out_specs=pl.BlockSpec((rows, tn), lambda j: (0, j)),
        out_shape=jax.ShapeDtypeStruct((rows, n), F32),
        compiler_params=_cparams("arbitrary"),
        name="ada",
    )(cp, w, b.reshape(1, n))
    return out[:bsz]


def _conv_window(win, first, w_ref, b_ref, g_ref, beta_ref, ts, span):
    sub = 8
    base = CONV_HALO - (span - 1)
    acc = None
    for q in range(sub):
        part = None
        for k in range(span):
            if (base + k) % sub == q:
                lo = pl.multiple_of(first + (base + k - q), sub)
                term = win[pl.ds(lo, ts + sub), :] * w_ref[k:k + 1, :]
                part = term if part is None else part + term
        if part is not None:
            part = part[q:q + ts, :]
            acc = part if acc is None else acc + part
    y = _ln(acc + b_ref[...]) * g_ref[...] + beta_ref[...]
    return y * jax.nn.sigmoid(y)


def _proj_kernel(x_ref, shift_ref, scale_ref, w_ref, b_ref, wdw_ref, bdw_ref, lng_ref, lnb_ref,
                 u_ref, c_ref, g_ref, h_scr, a_scr, win, *, n_u, n_a, n_g, span, tiles_per_batch):
    i, j = pl.program_id(0), pl.program_id(1)
    tm, tn = x_ref.shape[0], w_ref.shape[1]
    g0 = n_u + 2 * n_a
    piece = tm // n_g

    @pl.when(j == 0)
    def _():
        h = _ln(x_ref[...]) * (1.0 + scale_ref[0]) + shift_ref[0]
        h_scr[...] = h.astype(BF16)

        @pl.when(i == 0)
        def _():
            win[...] = jnp.zeros(win.shape, F32)

        @pl.when(i % tiles_per_batch == 0)
        def _():
            win[0:CONV_HALO, :] = jnp.zeros((CONV_HALO, win.shape[1]), F32)

    p = _dot(h_scr[...], w_ref[...]) + b_ref[...]

    first = pl.multiple_of(jnp.clip(j - g0, 0, n_g - 1) * piece, piece)
    act = _conv_window(win, first, wdw_ref, bdw_ref, lng_ref, lnb_ref, piece, span)
    c_ref[pl.ds(first, piece), :] = act.astype(BF16)

    @pl.when(j < n_u)
    def _():
        for q in range(u_ref.shape[0]):
            u_ref[q] = p[:, q * LANES:(q + 1) * LANES]

    @pl.when((j >= n_u) & (j < n_u + n_a))
    def _():
        a_scr[j - n_u] = p

    for aj in range(n_a):
        @pl.when(j == n_u + n_a + aj)
        def _():
            win[CONV_HALO:CONV_HALO + tm, aj * tn:(aj + 1) * tn] = a_scr[aj] * _sigmoid(p)

    @pl.when(j >= g0)
    def _():
        g_ref[...] = _sigmoid(p).astype(BF16)

    @pl.when(j == g0 + n_g - 1)
    def _():
        win[0:CONV_HALO, :] = win[tm:tm + CONV_HALO, :]


def _proj(x2, shift, scale, w_bf, b, w_dw, b_dw, ln_g, ln_b, s5_width, seq, tm, tn):
    t, d = x2.shape
    n = w_bf.shape[1]
    span, conv_ch = w_dw.shape[0], w_dw.shape[-1]
    n_u, n_a = s5_width // tn, conv_ch // tn
    assert n_u == 1, "the S5 input must be one column tile"
    n_lt = s5_width // LANES
    n_g = (n - s5_width - 2 * conv_ch) // tn
    tiles_per_batch = seq // tm
    g0 = n_u + 2 * n_a
    row = lambda a: a.reshape(1, conv_ch)
    const = lambda a: pl.BlockSpec(a.shape, lambda i, j: (0, 0))
    w_dw, b_dw, ln_g, ln_b = w_dw.reshape(span, conv_ch), row(b_dw), row(ln_g), row(ln_b)
    kern = functools.partial(_proj_kernel, n_u=n_u, n_a=n_a, n_g=n_g, span=span, tiles_per_batch=tiles_per_batch)
    return pl.pallas_call(
        kern,
        grid=(t // tm, n // tn),
        in_specs=[pl.BlockSpec((tm, d), lambda i, j: (i, 0)),
                  pl.BlockSpec((1, 1, d), lambda i, j: (i // tiles_per_batch, 0, 0)),
                  pl.BlockSpec((1, 1, d), lambda i, j: (i // tiles_per_batch, 0, 0)),
                  pl.BlockSpec((d, tn), lambda i, j: (0, j)),
                  pl.BlockSpec((1, tn), lambda i, j: (0, j)),
                  const(w_dw), const(b_dw), const(ln_g), const(ln_b)],
        out_specs=[pl.BlockSpec((n_lt, tm, LANES), lambda i, j: (0, i, 0)),
                   pl.BlockSpec((tm, conv_ch), lambda i, j: (i, 0)),
                   pl.BlockSpec((tm, tn), lambda i, j: (i, jnp.clip(j - g0, 0, n_g - 1)))],
        out_shape=[jax.ShapeDtypeStruct((n_lt, t, LANES), F32),
                   jax.ShapeDtypeStruct((t, conv_ch), BF16),
                   jax.ShapeDtypeStruct((t, n_g * tn), BF16)],
        scratch_shapes=[pltpu.VMEM((tm, d), BF16), pltpu.VMEM((n_a, tm, tn), F32),
                        pltpu.VMEM((CONV_HALO + tm + 8, conv_ch), F32)],
        compiler_params=_cparams("arbitrary", "arbitrary"),
        name="proj",
    )(x2, shift, scale, w_bf, b.reshape(1, n), w_dw, b_dw, ln_g, ln_b)


def _s5_tables(a_re, a_im, log_dt, b_re, b_im, c_re, c_im, d_skip, chunk):
    hp = lax.Precision.HIGHEST
    g, p = a_re.shape
    hw = b_re.shape[-1]
    w = chunk * hw
    dt = jnp.exp(log_dt.astype(F32))[:, None, None]
    lr, li = a_re.astype(F32)[:, :, None], a_im.astype(F32)[:, :, None]

    def power(k):
        mag = jnp.exp(k * lr * dt)
        return mag * jnp.cos(k * li * dt), mag * jnp.sin(k * li * dt)

    a1_re, a1_im = power(1.0)
    den = lr * lr + li * li
    nr, ni = a1_re - 1.0, a1_im
    z_re = (nr * lr + ni * li) / den
    z_im = (ni * lr - nr * li) / den
    br, bi = b_re.astype(F32), b_im.astype(F32)
    bb_re = z_re * br - z_im * bi
    bb_im = z_re * bi + z_im * br
    lag = (jnp.arange(w + hw) // hw).astype(F32)[None, None, :]
    pw_re, pw_im = power(lag)
    ct_re = jnp.tile(c_re.astype(F32).transpose(0, 2, 1), (1, 1, chunk + 1))
    ct_im = jnp.tile(c_im.astype(F32).transpose(0, 2, 1), (1, 1, chunk + 1))
    ca_re = ct_re * pw_re - ct_im * pw_im
    ca_im = ct_re * pw_im + ct_im * pw_re
    kt = (jnp.einsum('gpj,gpl->gjl', bb_re, ca_re[:, :, :w], precision=hp)
          - jnp.einsum('gpj,gpl->gjl', bb_im, ca_im[:, :, :w], precision=hp))
    m = jnp.stack([jnp.pad(kt[:, :, :w - s * hw], ((0, 0), (0, 0), (s * hw, 0))) for s in range(chunk)],
                  axis=1).reshape(g, w, w)
    rev = (chunk - 1 - jnp.arange(w) // hw).astype(F32)[None, None, :]
    rv_re, rv_im = power(rev)
    bt_re, bt_im = jnp.tile(bb_re, (1, 1, chunk)), jnp.tile(bb_im, (1, 1, chunk))
    pt_re = rv_re * bt_re - rv_im * bt_im
    pt_im = rv_re * bt_im + rv_im * bt_re
    qo_re, qo_im = ca_re[:, :, hw:], -ca_im[:, :, hw:]
    al_re, al_im = power(float(chunk))
    al_re, al_im = al_re.reshape(g, 1, p), al_im.reshape(g, 1, p)
    d_t = jnp.tile(d_skip.astype(F32)[:, None, :], (1, 1, chunk))
    return m, pt_re, pt_im, qo_re, qo_im, al_re, al_im, d_t


def _s5_kernel(u_ref, m_ref, pre_ref, pim_ref, qre_ref, qim_ref, are_ref, aim_ref, d_ref, y_ref,
               zre, zim, sre, sim, *, bsz):
    u = u_ref[0]
    rows = u.shape[0]
    nc = rows // bsz
    y = _dot(u, m_ref[0]) + d_ref[0] * u
    nt = (((1,), (1,)), ((), ()))
    zre[...] = lax.dot_general(u, pre_ref[0], nt, preferred_element_type=F32)
    zim[...] = lax.dot_general(u, pim_ref[0], nt, preferred_element_type=F32)
    ar, ai = are_ref[0], aim_ref[0]
    npair = zre.shape[1]

    def step(c, carry):
        nxt = []
        for b in range(bsz):
            sr, si = carry[2 * b], carry[2 * b + 1]
            row = pl.ds(b * nc + c, 1)
            sre[row, :] = sr
            sim[row, :] = si
            nxt.append(ar * sr - ai * si + zre[row, :])
            nxt.append(ar * si + ai * sr + zim[row, :])
        return tuple(nxt)

    zero = jnp.zeros((1, npair), F32)
    lax.fori_loop(0, nc, step, tuple(zero for _ in range(2 * bsz)))
    y = y + _dot(sre[...], qre_ref[0]) + _dot(sim[...], qim_ref[0])
    y_ref[0] = jax.nn.gelu(y, approximate=True)


def _pack_kernel(u_ref, o_ref, *, chunk):
    g, cb, _ = o_ref.shape
    n_lt = u_ref.shape[0]
    per = g // n_lt
    hw = LANES // per
    for s in range(chunk):
        for q in range(n_lt):
            rows = u_ref[q, pl.ds(s, cb, stride=chunk), :]
            for i in range(per):
                o_ref[q * per + i, :, s * hw:(s + 1) * hw] = rows[:, i * hw:(i + 1) * hw]


def _unpack_kernel(y_ref, o_ref, tmp, *, chunk):
    g, cb, _ = y_ref.shape
    n_lt = o_ref.shape[0]
    per = g // n_lt
    hw = LANES // per
    for s in range(chunk):
        for q in range(n_lt):
            for i in range(per):
                tmp[:, i * hw:(i + 1) * hw] = y_ref[q * per + i, :, s * hw:(s + 1) * hw]
            o_ref[q, pl.ds(s, cb, stride=chunk), :] = tmp[...]


def _s5(u, tables, bsz, chunk):
    m, p_re, p_im, q_re, q_im, a_re, a_im, d_t = tables
    n_lt, t, _ = u.shape
    g, cw, _ = m.shape
    npair = p_re.shape[1]
    rows = t // chunk
    cb = min(64, rows)
    by_chunk = pl.BlockSpec((n_lt, cb * chunk, LANES), lambda i: (0, i, 0))
    by_group = pl.BlockSpec((g, cb, cw), lambda i: (0, i, 0))
    ut = pl.pallas_call(
        functools.partial(_pack_kernel, chunk=chunk), grid=(rows // cb,), in_specs=[by_chunk],
        out_specs=by_group, out_shape=jax.ShapeDtypeStruct((g, rows, cw), F32),
        compiler_params=_cparams("arbitrary"), name="s5_pack",
    )(u)
    gspec = lambda *shape: pl.BlockSpec((1,) + shape, lambda i: (i, 0, 0))
    yt = pl.pallas_call(
        functools.partial(_s5_kernel, bsz=bsz),
        grid=(g,),
        in_specs=[gspec(rows, cw), gspec(cw, cw), gspec(npair, cw), gspec(npair, cw),
                  gspec(npair, cw), gspec(npair, cw), gspec(1, npair), gspec(1, npair), gspec(1, cw)],
        out_specs=gspec(rows, cw),
        out_shape=jax.ShapeDtypeStruct((g, rows, cw), F32),
        scratch_shapes=[pltpu.VMEM((rows, npair), F32) for _ in range(4)],
        compiler_params=_cparams("arbitrary"),
        name="s5",
    )(ut, m, p_re, p_im, q_re, q_im, a_re, a_im, d_t)
    return pl.pallas_call(
        functools.partial(_unpack_kernel, chunk=chunk), grid=(rows // cb,), in_specs=[by_group],
        out_specs=by_chunk, out_shape=jax.ShapeDtypeStruct((n_lt, t, LANES), F32),
        scratch_shapes=[pltpu.VMEM((cb, LANES), F32)],
        compiler_params=_cparams("arbitrary"), name="s5_unpack",
    )(yt)


def _branch_kernel(ys_ref, ca_ref, gs_ref, wgate_ref, wup_ref, wco_ref, o_ref):
    ys = jnp.concatenate([ys_ref[q] for q in range(ys_ref.shape[0])], axis=-1)
    glu = ys * jax.nn.sigmoid(_dot(ys.astype(BF16), wgate_ref[...]))
    y_s5 = _dot(glu.astype(BF16), wup_ref[...])
    y_conv = _dot(ca_ref[...], wco_ref[...])
    d = y_s5.shape[1]
    merged = gs_ref[:, :d].astype(F32) * y_s5 + gs_ref[:, d:].astype(F32) * y_conv
    o_ref[...] = merged.astype(BF16)


def _branch(ys, cact, gates, w_gate, w_up, w_co, tm):
    n_lt, t, _ = ys.shape
    cc = cact.shape[1]
    d = w_up.shape[1]
    full = lambda a: pl.BlockSpec(a.shape, lambda i: (0, 0))
    return pl.pallas_call(
        _branch_kernel,
        grid=(t // tm,),
        in_specs=[pl.BlockSpec((n_lt, tm, LANES), lambda i: (0, i, 0)),
                  pl.BlockSpec((tm, cc), lambda i: (i, 0)),
                  pl.BlockSpec((tm, 2 * d), lambda i: (i, 0)),
                  full(w_gate), full(w_up), full(w_co)],
        out_specs=pl.BlockSpec((tm, d), lambda i: (i, 0)),
        out_shape=jax.ShapeDtypeStruct((t, d), BF16),
        compiler_params=_cparams("arbitrary"),
        name="branch",
    )(ys, cact, gates, w_gate, w_up, w_co)


def _mix_kernel(m_ref, x_ref, gate_ref, g1_ref, b1_ref, scale2_ref, shift2_ref, wout_ref, wr_ref, br_ref,
                x1_ref, h2_ref, route_ref, routet_ref, cnt_ref, run, *, alpha, n_exp, n_grp):
    @pl.when(pl.program_id(0) == 0)
    def _():
        run[...] = jnp.zeros_like(run)

    mix = _dot(m_ref[...], wout_ref[...])
    x1 = _ln(alpha * x_ref[...] + (1.0 + gate_ref[0]) * mix) * g1_ref[...] + b1_ref[...]
    x1_ref[...] = x1
    h2 = _ln(x1) * (1.0 + scale2_ref[0]) + shift2_ref[0]
    _store_rows(h2_ref, 0, h2.shape[0], h2.shape[1] // (2 * LANES), _pack_pairs(h2))

    logits = _dot(h2, wr_ref[...]) + br_ref[...]
    tm = logits.shape[0]
    per = n_exp // n_grp
    lane = lax.broadcasted_iota(jnp.int32, logits.shape, 1)
    big = jnp.int32(LANES)
    neg = jnp.float32(-jnp.inf)

    def first_max(val):
        top = jnp.max(val, axis=-1, keepdims=True)
        return top, jnp.min(jnp.where(val == top, lane, big), axis=-1, keepdims=True)

    gmask = (lane >= n_exp) & (lane < n_exp + n_grp)
    gtop, glane = first_max(jnp.where(gmask, logits, neg))
    grp = glane - n_exp
    grp_w = 1.0 / jnp.sum(jnp.where(gmask, jnp.exp(logits - gtop), 0.0), axis=-1, keepdims=True)
    emask = (lane >= grp * per) & (lane < grp * per + per)
    el = jnp.where(emask, logits, neg)
    t1, e1 = first_max(el)
    t2, e2 = first_max(jnp.where(lane == e1, neg, el))
    ex = jnp.exp(t2 - t1)
    w1 = grp_w / (1.0 + ex)
    w2 = grp_w * ex / (1.0 + ex)

    oh1 = (lane == e1).astype(F32)
    oh2 = (lane == e2).astype(F32)
    both = oh1 + oh2
    r_i = lax.broadcasted_iota(jnp.int32, (tm, tm), 0)
    c_i = lax.broadcasted_iota(jnp.int32, (tm, tm), 1)
    before = _dot((r_i > c_i).astype(F32), both) + run[...]
    rank1 = jnp.sum(before * oh1, axis=-1, keepdims=True)
    rank2 = jnp.sum(before * oh2, axis=-1, keepdims=True)
    run[...] = run[...] + jnp.sum(both, axis=0, keepdims=True)
    cnt_ref[...] = run[...]

    route = jnp.where(lane == 0, e1.astype(F32), 0.0)
    route = jnp.where(lane == 1, e2.astype(F32), route)
    route = jnp.where(lane == 2, w1, route)
    route = jnp.where(lane == 3, w2, route)
    route = jnp.where(lane == 4, rank1, route)
    route = jnp.where(lane == 5, rank2, route)
    route_ref[...] = route
    routet_ref[...] = route.T[0:routet_ref.shape[0], :]


def _mix(merged, x2, gate1, g1, b1, scale2, shift2, w_out, w_r, b_r, seq, tm, alpha, n_exp, n_grp):
    t, d = x2.shape
    k = d // (2 * LANES)
    tiles_per_batch = seq // tm
    tile = pl.BlockSpec((tm, d), lambda i: (i, 0))
    per_batch = pl.BlockSpec((1, 1, d), lambda i: (i // tiles_per_batch, 0, 0))
    const = lambda a: pl.BlockSpec(a.shape, lambda i: (0, 0))
    g1, b1 = g1.reshape(1, d), b1.reshape(1, d)
    kern = functools.partial(_mix_kernel, alpha=alpha, n_exp=n_exp, n_grp=n_grp)
    return pl.pallas_call(
        kern,
        grid=(t // tm,),
        in_specs=[tile, tile, per_batch, const(g1), const(b1), per_batch, per_batch,
                  const(w_out), const(w_r), const(b_r)],
        out_specs=[tile, pl.BlockSpec((tm * k, LANES), lambda i: (i, 0)),
                   pl.BlockSpec((tm, LANES), lambda i: (i, 0)),
                   pl.BlockSpec((8, tm), lambda i: (0, i)),
                   pl.BlockSpec((1, LANES), lambda i: (0, 0))],
        out_shape=[jax.ShapeDtypeStruct((t, d), F32), jax.ShapeDtypeStruct((t * k, LANES), jnp.uint32),
                   jax.ShapeDtypeStruct((t, LANES), F32), jax.ShapeDtypeStruct((8, t), F32),
                   jax.ShapeDtypeStruct((1, LANES), F32)],
        scratch_shapes=[pltpu.VMEM((1, LANES), F32)],
        compiler_params=_cparams("arbitrary"),
        name="mix_route",
    )(merged, x2, gate1, g1, b1, scale2, shift2, w_out, w_r, b_r)


def _moe_kernel(bexp_ref, nused_ref, nvalid_ref, tok_ref, tok_next_ref, slot_ref, h_hbm, wg_ref, wu_ref, wd_ref,
                y_hbm, xbuf, ybuf, gsem, ssem):
    del bexp_ref
    rows = tok_ref.shape[2]
    k = xbuf.shape[1] // rows
    group = MOE_DMA_GROUP
    b = pl.program_id(0)
    last = pl.num_programs(0) - 1
    n_used = nused_ref[0]
    cur, nxt = b % 2, (b + 1) % 2
    nv_cur = nvalid_ref[b]
    nv_next = nvalid_ref[jnp.minimum(b + 1, last)]

    def per_group(n_valid, fn):
        for g0 in range(0, rows, group):
            pl.when(g0 < n_valid)(functools.partial(fn, g0))

    def gather_start(idx_ref, buf, n_valid):
        def start(g0):
            for r in range(g0, g0 + group):
                src = h_hbm.at[pl.ds(pl.multiple_of(idx_ref[0, 0, r], k), k)]
                pltpu.make_async_copy(src, xbuf.at[buf, pl.ds(r * k, k)], gsem.at[buf]).start()
        per_group(n_valid, start)

    def gather_wait(buf, n_valid):
        def wait(g0):
            pltpu.make_async_copy(h_hbm.at[pl.ds(0, group * k)], xbuf.at[buf, pl.ds(g0 * k, group * k)],
                                  gsem.at[buf]).wait()
        per_group(n_valid, wait)

    def scatter_start(buf, n_valid):
        def start(g0):
            for r in range(g0, g0 + group):
                dst = y_hbm.at[pl.ds(pl.multiple_of(slot_ref[0, 0, r], k), k)]
                pltpu.make_async_copy(ybuf.at[buf, pl.ds(r * k, k)], dst, ssem.at[buf]).start()
        per_group(n_valid, start)

    def scatter_wait(buf, n_valid):
        def wait(g0):
            pltpu.make_async_copy(ybuf.at[buf, pl.ds(g0 * k, group * k)], y_hbm.at[pl.ds(0, group * k)],
                                  ssem.at[buf]).wait()
        per_group(n_valid, wait)

    @pl.when(b == 0)
    def _():
        xbuf[...] = jnp.zeros(xbuf.shape, xbuf.dtype)
        ybuf[1] = jnp.zeros(ybuf.shape[1:], ybuf.dtype)
        spare = pltpu.make_async_copy(ybuf.at[1], y_hbm.at[pl.ds(y_hbm.shape[0] - rows * k, rows * k)], ssem.at[1])
        spare.start()
        spare.wait()
        gather_start(tok_ref, 0, nv_cur)

    @pl.when((b >= 2) & (b < n_used))
    def _():
        scatter_wait(cur, nvalid_ref[jnp.maximum(b - 2, 0)])

    @pl.when(b < n_used)
    def _():
        gather_start(tok_next_ref, nxt, nv_next)
        gather_wait(cur, nv_cur)
        x = _unpack_pairs(_load_rows(xbuf.at[cur], 0, rows, k, k))
        hg = _dot(x, wg_ref[0])
        act = hg * jax.nn.sigmoid(hg) * _dot(x, wu_ref[0])
        _store_rows(ybuf.at[cur], 0, rows, k, _pack_pairs(_dot(act, wd_ref[0])))
        scatter_start(cur, nv_cur)

    @pl.when(b == n_used - 1)
    def _():
        gather_wait(nxt, nv_next)
        scatter_wait(cur, nv_cur)

        @pl.when(b >= 1)
        def _():
            scatter_wait(nxt, nvalid_ref[jnp.maximum(b - 1, 0)])


def _moe(h2, row_tok, row_slot, block_expert, n_used, n_valid, w_g, w_u, w_d, n_slots):
    de, d = w_d.shape[-2:]
    k = d // (2 * LANES)
    n_blocks = row_tok.shape[0]
    smem = lambda off: pl.BlockSpec((1, 1, MOE_ROWS),
                                    lambda b, e, n, v: (jnp.minimum(b + off, n_blocks - 1), 0, 0),
                                    memory_space=pltpu.SMEM)
    grid_spec = pltpu.PrefetchScalarGridSpec(
        num_scalar_prefetch=3,
        grid=(n_blocks,),
        in_specs=[smem(0), smem(1), smem(0),
                  pl.BlockSpec(memory_space=pl.ANY),
                  pl.BlockSpec((1, d, de), lambda b, e, n, v: (e[b], 0, 0)),
                  pl.BlockSpec((1, d, de), lambda b, e, n, v: (e[b], 0, 0)),
                  pl.BlockSpec((1, de, d), lambda b, e, n, v: (e[b], 0, 0))],
        out_specs=pl.BlockSpec(memory_space=pl.ANY),
        scratch_shapes=[pltpu.VMEM((2, MOE_ROWS * k, LANES), jnp.uint32),
                        pltpu.VMEM((2, MOE_ROWS * k, LANES), jnp.uint32),
                        pltpu.SemaphoreType.DMA((2,)), pltpu.SemaphoreType.DMA((2,))],
    )
    return pl.pallas_call(
        _moe_kernel,
        grid_spec=grid_spec,
        out_shape=jax.ShapeDtypeStruct(((n_slots + MOE_ROWS) * k, LANES), jnp.uint32),
        compiler_params=_cparams("arbitrary"),
        name="moe",
    )(block_expert, n_used, n_valid, row_tok, row_tok, row_slot, h2, w_g, w_u, w_d)


def _dispatch_tables(route_t, counts, n_exp, k):
    t = route_t.shape[1]
    n_slots = t * TOP_K
    n_blocks = n_slots // MOE_ROWS + n_exp
    slot_e = route_t[0:TOP_K].astype(jnp.int32)
    rank = route_t[4:4 + TOP_K].astype(jnp.int32)
    cnt = counts[0, :n_exp].astype(jnp.int32)
    padded = (cnt + MOE_ROWS - 1) // MOE_ROWS * MOE_ROWS
    pends = jnp.cumsum(padded)
    pstarts = pends - padded
    experts = jnp.arange(n_exp, dtype=jnp.int32)[:, None, None]
    dest = (jnp.sum(jnp.where(slot_e[None] == experts, pstarts[:, None, None], 0), axis=0) + rank).reshape(-1)
    slot_id = (TOP_K * jnp.arange(t, dtype=jnp.int32)[None, :] + jnp.arange(TOP_K, dtype=jnp.int32)[:, None])
    spare = n_slots + jnp.arange(n_blocks * MOE_ROWS, dtype=jnp.int32) % MOE_ROWS
    row_slot = spare.at[dest].set(slot_id.reshape(-1))
    row_tok = jnp.where(row_slot < n_slots, row_slot // TOP_K, 0)
    block_start = jnp.arange(n_blocks, dtype=jnp.int32) * MOE_ROWS
    block_expert = jnp.minimum(jnp.searchsorted(pends, block_start, side='right'), n_exp - 1).astype(jnp.int32)
    n_valid = jnp.clip((pstarts + cnt)[block_expert] - block_start, 0, MOE_ROWS).astype(jnp.int32)
    n_used = (pends[-1:] // MOE_ROWS).astype(jnp.int32)
    shape3 = (n_blocks, 1, MOE_ROWS)
    return (row_tok * k).reshape(shape3), (row_slot * k).reshape(shape3), block_expert, n_used, n_valid


def _final_kernel(y_ref, route_ref, x1_ref, gate_ref, g_ref, b_ref, o_ref, *, alpha):
    tm, d = x1_ref.shape
    k = d // (2 * LANES)
    route = route_ref[...]
    y0 = _unpack_pairs(_load_rows(y_ref, 0, tm, k, TOP_K * k))
    y1 = _unpack_pairs(_load_rows(y_ref, k, tm, k, TOP_K * k))
    ffn = route[:, 2:3] * y0 + route[:, 3:4] * y1
    o_ref[...] = _ln(alpha * x1_ref[...] + (1.0 + gate_ref[0]) * ffn) * g_ref[...] + b_ref[...]


def _final(y_slots, route, x1, gate2, g2, b2, seq, tm, alpha):
    t, d = x1.shape
    k = d // (2 * LANES)
    tiles_per_batch = seq // tm
    return pl.pallas_call(
        functools.partial(_final_kernel, alpha=alpha),
        grid=(t // tm,),
        in_specs=[pl.BlockSpec((tm * TOP_K * k, LANES), lambda i: (i, 0)),
                  pl.BlockSpec((tm, LANES), lambda i: (i, 0)),
                  pl.BlockSpec((tm, d), lambda i: (i, 0)),
                  pl.BlockSpec((1, 1, d), lambda i: (i // tiles_per_batch, 0, 0)),
                  pl.BlockSpec((1, d), lambda i: (0, 0)),
                  pl.BlockSpec((1, d), lambda i: (0, 0))],
        out_specs=pl.BlockSpec((tm, d), lambda i: (i, 0)),
        out_shape=jax.ShapeDtypeStruct((t, d), F32),
        compiler_params=_cparams("arbitrary"),
        name="final",
    )(y_slots, route, x1, gate2, g2.reshape(1, d), b2.reshape(1, d))


def _tiles(seq):
    return dict(proj=min(1024, seq), branch=min(256, seq), mix=min(256, seq), final=min(512, seq))


def kernel(x, c, w_ada, b_ada, w_in, b_in, s5_a_re, s5_a_im, s5_log_dt, s5_b_re, s5_b_im, s5_c_re, s5_c_im, s5_d, w_s5_gate, w_s5_up, conv_dw, conv_dw_b, conv_ln_g, conv_ln_b, w_conv_out, w_out, ln1_g, ln1_b, w_route_group, b_route_group, w_route_expert, b_route_expert, w_exp_gate, w_exp_up, w_exp_down, ln2_g, ln2_b):
    bsz, seq, d = x.shape
    t = bsz * seq
    depth = w_ada.shape[0]
    alpha = (2.0 * depth) ** 0.25
    s5_width = w_s5_gate.shape[1]
    conv_ch = conv_dw.shape[-1]
    n_grp = w_route_group.shape[-1]
    n_exp = w_route_expert.shape[-1]
    tl = _tiles(seq)
    tn = min(512, s5_width)
    x2 = x.reshape(t, d)
    for l in range(depth):
        mod = _ada(c, w_ada[l], b_ada[l])
        shift1, scale1, gate1, shift2, scale2, gate2 = [
            m.reshape(bsz, 1, d) for m in jnp.split(mod, 6, axis=-1)]

        u, cact, gates = _proj(x2, shift1, scale1, w_in[l].astype(BF16), b_in[l], conv_dw[l], conv_dw_b[l],
                               conv_ln_g[l], conv_ln_b[l], s5_width, seq, tl["proj"], tn)
        tables = _s5_tables(s5_a_re[l], s5_a_im[l], s5_log_dt[l], s5_b_re[l], s5_b_im[l],
                            s5_c_re[l], s5_c_im[l], s5_d[l], S5_CHUNK)
        ys = _s5(u, tables, bsz, S5_CHUNK)
        merged = _branch(ys, cact, gates, w_s5_gate[l].astype(BF16), w_s5_up[l].astype(BF16),
                         w_conv_out[l].astype(BF16), tl["branch"])

        w_r = jnp.zeros((d, LANES), F32).at[:, :n_exp].set(w_route_expert[l])
        w_r = w_r.at[:, n_exp:n_exp + n_grp].set(w_route_group[l])
        b_r = jnp.zeros((1, LANES), F32).at[0, :n_exp].set(b_route_expert[l])
        b_r = b_r.at[0, n_exp:n_exp + n_grp].set(b_route_group[l])
        x1, h2, route, route_t, counts = _mix(merged, x2, gate1, ln1_g[l], ln1_b[l], scale2, shift2,
                                              w_out[l].astype(BF16), w_r, b_r, seq, tl["mix"], alpha, n_exp, n_grp)
        row_tok, row_slot, block_expert, n_used, n_valid = _dispatch_tables(route_t, counts, n_exp,
                                                                            d // (2 * LANES))
        y_slots = _moe(h2, row_tok, row_slot, block_expert, n_used, n_valid, w_exp_gate[l], w_exp_up[l],
                       w_exp_down[l], t * TOP_K)
        x2 = _final(y_slots, route, x1, gate2, ln2_g[l], ln2_b[l], seq, tl["final"], alpha)
    return x2.reshape(bsz, seq, d)
```

```python
import functools

import jax
import jax.numpy as jnp
from jax import lax
from jax.experimental import pallas as pl
from jax.experimental.pallas import tpu as pltpu

F32 = jnp.float32
BF16 = jnp.bfloat16
LN_EPS = 1e-5
TOP_K = 2
LANES = 128
S5_CHUNK = 16
CONV_HALO = 32
MOE_ROWS = 256
MOE_DMA_GROUP = 32
VMEM_LIMIT = 56 * 1024 * 1024


def _cparams(*sem):
    return pltpu.CompilerParams(dimension_semantics=sem, vmem_limit_bytes=VMEM_LIMIT)


def _ln(x):
    mu = jnp.mean(x, axis=-1, keepdims=True)
    xc = x - mu
    var = jnp.mean(xc * xc, axis=-1, keepdims=True)
    return xc * lax.rsqrt(var + LN_EPS)


def _dot(a, b):
    return jnp.dot(a, b, preferred_element_type=F32)


def _sigmoid(x):
    return 0.5 * jnp.tanh(0.5 * x) + 0.5


def _pack_pairs(a):
    w = a.shape[1] // 2
    hi = lax.bitcast_convert_type(a[:, :w].astype(BF16).astype(F32), jnp.uint32)
    lo = lax.bitcast_convert_type(a[:, w:].astype(BF16).astype(F32), jnp.uint32)
    return hi | (lo >> 16)


def _unpack_pairs(words):
    hi = lax.bitcast_convert_type(words & jnp.uint32(0xFFFF0000), F32)
    lo = lax.bitcast_convert_type(words << 16, F32)
    return jnp.concatenate([hi, lo], axis=-1)


def _store_rows(ref, first, n, k, packed):
    for j in range(k):
        ref[pl.ds(first + j, n, stride=k), :] = packed[:, j * LANES:(j + 1) * LANES]


def _load_rows(ref, first, n, k, stride):
    return jnp.concatenate([ref[pl.ds(first + j, n, stride=stride), :] for j in range(k)], axis=-1)


def _ada_kernel(c_ref, w_ref, b_ref, o_ref):
    c = c_ref[...]
    o_ref[...] = _dot(c * jax.nn.sigmoid(c), w_ref[...]) + b_ref[...]


def _ada(c, w, b, tn=1024):
    bsz, d = c.shape
    n = w.shape[1]
    rows = 8
    cp = jnp.zeros((rows, d), F32).at[:bsz].set(c)
    out = pl.pallas_call(
        _ada_kernel,
        grid=(n // tn,),
        in_specs=[pl.BlockSpec((rows, d), lambda j: (0, 0)),
                  pl.BlockSpec((d, tn), lambda j: (0, j)),
                  pl.BlockSpec((1, tn), lambda j: (0, j))],
        out_specs=pl.BlockSpec((rows, tn), lambda j: (0, j)),
        out_shape=jax.ShapeDtypeStruct((rows, n), F32),
        compiler_params=_cparams("arbitrary"),
        name="ada",
    )(cp, w, b.reshape(1, n))
    return out[:bsz]


def _proj_kernel(x_ref, shift_ref, scale_ref, w_ref, b_ref, u_ref, v_ref, g_ref, h_scr, a_scr,
                 *, n_u, n_a):
    j = pl.program_id(1)

    @pl.when(j == 0)
    def _():
        h = _ln(x_ref[...]) * (1.0 + scale_ref[0]) + shift_ref[0]
        h_scr[...] = h.astype(BF16)

    p = _dot(h_scr[...], w_ref[...]) + b_ref[...]

    @pl.when(j < n_u)
    def _():
        for q in range(u_ref.shape[0]):
            u_ref[q] = p[:, q * LANES:(q + 1) * LANES]

    @pl.when((j >= n_u) & (j < n_u + n_a))
    def _():
        a_scr[j - n_u] = p

    @pl.when((j >= n_u + n_a) & (j < n_u + 2 * n_a))
    def _():
        v_ref[...] = a_scr[j - n_u - n_a] * _sigmoid(p)

    @pl.when(j >= n_u + 2 * n_a)
    def _():
        g_ref[...] = _sigmoid(p).astype(BF16)


def _proj(x2, shift, scale, w_bf, b, s5_width, conv_ch, seq, tm, tn):
    t, d = x2.shape
    n = w_bf.shape[1]
    n_u, n_a = s5_width // tn, conv_ch // tn
    assert n_u == 1, "the S5 input must be one column tile"
    n_lt = s5_width // LANES
    n_g = (n - s5_width - 2 * conv_ch) // tn
    tiles_per_batch = seq // tm
    g0 = n_u + 2 * n_a
    kern = functools.partial(_proj_kernel, n_u=n_u, n_a=n_a)
    return pl.pallas_call(
        kern,
        grid=(t // tm, n // tn),
        in_specs=[pl.BlockSpec((tm, d), lambda i, j: (i, 0)),
                  pl.BlockSpec((1, 1, d), lambda i, j: (i // tiles_per_batch, 0, 0)),
                  pl.BlockSpec((1, 1, d), lambda i, j: (i // tiles_per_batch, 0, 0)),
                  pl.BlockSpec((d, tn), lambda i, j: (0, j)),
                  pl.BlockSpec((1, tn), lambda i, j: (0, j))],
        out_specs=[pl.BlockSpec((n_lt, tm, LANES), lambda i, j: (0, i, 0)),
                   pl.BlockSpec((tm, tn), lambda i, j: (i, jnp.clip(j - n_u - n_a, 0, n_a - 1))),
                   pl.BlockSpec((tm, tn), lambda i, j: (i, jnp.clip(j - g0, 0, n_g - 1)))],
        out_shape=[jax.ShapeDtypeStruct((n_lt, t, LANES), F32),
                   jax.ShapeDtypeStruct((t, conv_ch), F32),
                   jax.ShapeDtypeStruct((t, n_g * tn), BF16)],
        scratch_shapes=[pltpu.VMEM((tm, d), BF16), pltpu.VMEM((n_a, tm, tn), F32)],
        compiler_params=_cparams("arbitrary", "arbitrary"),
        name="proj",
    )(x2, shift, scale, w_bf, b.reshape(1, n))


def _s5_tables(a_re, a_im, log_dt, b_re, b_im, c_re, c_im, d_skip, chunk):
    hp = lax.Precision.HIGHEST
    g, p = a_re.shape
    hw = b_re.shape[-1]
    w = chunk * hw
    dt = jnp.exp(log_dt.astype(F32))[:, None, None]
    lr, li = a_re.astype(F32)[:, :, None], a_im.astype(F32)[:, :, None]

    def power(k):
        mag = jnp.exp(k * lr * dt)
        return mag * jnp.cos(k * li * dt), mag * jnp.sin(k * li * dt)

    a1_re, a1_im = power(1.0)
    den = lr * lr + li * li
    nr, ni = a1_re - 1.0, a1_im
    z_re = (nr * lr + ni * li) / den
    z_im = (ni * lr - nr * li) / den
    br, bi = b_re.astype(F32), b_im.astype(F32)
    bb_re = z_re * br - z_im * bi
    bb_im = z_re * bi + z_im * br
    lag = (jnp.arange(w + hw) // hw).astype(F32)[None, None, :]
    pw_re, pw_im = power(lag)
    ct_re = jnp.tile(c_re.astype(F32).transpose(0, 2, 1), (1, 1, chunk + 1))
    ct_im = jnp.tile(c_im.astype(F32).transpose(0, 2, 1), (1, 1, chunk + 1))
    ca_re = ct_re * pw_re - ct_im * pw_im
    ca_im = ct_re * pw_im + ct_im * pw_re
    kt = (jnp.einsum('gpj,gpl->gjl', bb_re, ca_re[:, :, :w], precision=hp)
          - jnp.einsum('gpj,gpl->gjl', bb_im, ca_im[:, :, :w], precision=hp))
    m = jnp.stack([jnp.pad(kt[:, :, :w - s * hw], ((0, 0), (0, 0), (s * hw, 0))) for s in range(chunk)],
                  axis=1).reshape(g, w, w)
    rev = (chunk - 1 - jnp.arange(w) // hw).astype(F32)[None, None, :]
    rv_re, rv_im = power(rev)
    bt_re, bt_im = jnp.tile(bb_re, (1, 1, chunk)), jnp.tile(bb_im, (1, 1, chunk))
    pt_re = rv_re * bt_re - rv_im * bt_im
    pt_im = rv_re * bt_im + rv_im * bt_re
    qo_re, qo_im = ca_re[:, :, hw:], -ca_im[:, :, hw:]
    al_re, al_im = power(float(chunk))
    al_re, al_im = al_re.reshape(g, 1, p), al_im.reshape(g, 1, p)
    d_t = jnp.tile(d_skip.astype(F32)[:, None, :], (1, 1, chunk))
    return m, pt_re, pt_im, qo_re, qo_im, al_re, al_im, d_t


def _s5_kernel(u_ref, m_ref, pre_ref, pim_ref, qre_ref, qim_ref, are_ref, aim_ref, d_ref, y_ref,
               zre, zim, sre, sim, *, bsz):
    u = u_ref[0]
    rows = u.shape[0]
    nc = rows // bsz
    y = _dot(u, m_ref[0]) + d_ref[0] * u
    nt = (((1,), (1,)), ((), ()))
    zre[...] = lax.dot_general(u, pre_ref[0], nt, preferred_element_type=F32)
    zim[...] = lax.dot_general(u, pim_ref[0], nt, preferred_element_type=F32)
    ar, ai = are_ref[0], aim_ref[0]
    npair = zre.shape[1]

    def step(c, carry):
        nxt = []
        for b in range(bsz):
            sr, si = carry[2 * b], carry[2 * b + 1]
            row = pl.ds(b * nc + c, 1)
            sre[row, :] = sr
            sim[row, :] = si
            nxt.append(ar * sr - ai * si + zre[row, :])
            nxt.append(ar * si + ai * sr + zim[row, :])
        return tuple(nxt)

    zero = jnp.zeros((1, npair), F32)
    lax.fori_loop(0, nc, step, tuple(zero for _ in range(2 * bsz)))
    y = y + _dot(sre[...], qre_ref[0]) + _dot(sim[...], qim_ref[0])
    y_ref[0] = jax.nn.gelu(y, approximate=True)


def _pack_kernel(u_ref, o_ref, *, chunk):
    g, cb, _ = o_ref.shape
    n_lt = u_ref.shape[0]
    per = g // n_lt
    hw = LANES // per
    for s in range(chunk):
        for q in range(n_lt):
            rows = u_ref[q, pl.ds(s, cb, stride=chunk), :]
            for i in range(per):
                o_ref[q * per + i, :, s * hw:(s + 1) * hw] = rows[:, i * hw:(i + 1) * hw]


def _unpack_kernel(y_ref, o_ref, tmp, *, chunk):
    g, cb, _ = y_ref.shape
    n_lt = o_ref.shape[0]
    per = g // n_lt
    hw = LANES // per
    for s in range(chunk):
        for q in range(n_lt):
            for i in range(per):
                tmp[:, i * hw:(i + 1) * hw] = y_ref[q * per + i, :, s * hw:(s + 1) * hw]
            o_ref[q, pl.ds(s, cb, stride=chunk), :] = tmp[...]


def _s5(u, tables, bsz, chunk):
    m, p_re, p_im, q_re, q_im, a_re, a_im, d_t = tables
    n_lt, t, _ = u.shape
    g, cw, _ = m.shape
    npair = p_re.shape[1]
    rows = t // chunk
    cb = min(64, rows)
    by_chunk = pl.BlockSpec((n_lt, cb * chunk, LANES), lambda i: (0, i, 0))
    by_group = pl.BlockSpec((g, cb, cw), lambda i: (0, i, 0))
    ut = pl.pallas_call(
        functools.partial(_pack_kernel, chunk=chunk), grid=(rows // cb,), in_specs=[by_chunk],
        out_specs=by_group, out_shape=jax.ShapeDtypeStruct((g, rows, cw), F32),
        compiler_params=_cparams("arbitrary"), name="s5_pack",
    )(u)
    gspec = lambda *shape: pl.BlockSpec((1,) + shape, lambda i: (i, 0, 0))
    yt = pl.pallas_call(
        functools.partial(_s5_kernel, bsz=bsz),
        grid=(g,),
        in_specs=[gspec(rows, cw), gspec(cw, cw), gspec(npair, cw), gspec(npair, cw),
                  gspec(npair, cw), gspec(npair, cw), gspec(1, npair), gspec(1, npair), gspec(1, cw)],
        out_specs=gspec(rows, cw),
        out_shape=jax.ShapeDtypeStruct((g, rows, cw), F32),
        scratch_shapes=[pltpu.VMEM((rows, npair), F32) for _ in range(4)],
        compiler_params=_cparams("arbitrary"),
        name="s5",
    )(ut, m, p_re, p_im, q_re, q_im, a_re, a_im, d_t)
    return pl.pallas_call(
        functools.partial(_unpack_kernel, chunk=chunk), grid=(rows // cb,), in_specs=[by_group],
        out_specs=by_chunk, out_shape=jax.ShapeDtypeStruct((n_lt, t, LANES), F32),
        scratch_shapes=[pltpu.VMEM((cb, LANES), F32)],
        compiler_params=_cparams("arbitrary"), name="s5_unpack",
    )(yt)


def _conv_window(win, w_ref, b_ref, g_ref, beta_ref, ts, span):
    sub = 8
    base = CONV_HALO - (span - 1)
    acc = None
    for q in range(sub):
        part = None
        for k in range(span):
            if (base + k) % sub == q:
                lo = base + k - q
                term = win[lo:lo + ts + sub, :] * w_ref[k:k + 1, :]
                part = term if part is None else part + term
        if part is not None:
            part = part[q:q + ts, :]
            acc = part if acc is None else acc + part
    y = _ln(acc + b_ref[...]) * g_ref[...] + beta_ref[...]
    return y * jax.nn.sigmoid(y)


def _branch_kernel(ys_ref, v0_ref, vn_ref, gs_ref, wgate_ref, wup_ref, wco_ref, wdw_ref, bdw_ref, lng_ref,
                   lnb_ref, o_ref, win, cact, *, span, tiles_per_batch):
    i = pl.program_id(0)
    tm = o_ref.shape[0]
    zeros = functools.partial(jnp.zeros, dtype=F32)

    def conv_tile(v_ref, slot):
        win[CONV_HALO:CONV_HALO + tm, :] = v_ref[...]
        act = _conv_window(win, wdw_ref, bdw_ref, lng_ref, lnb_ref, tm, span)
        cact[slot] = act.astype(BF16)
        win[0:CONV_HALO, :] = win[tm:tm + CONV_HALO, :]

    @pl.when(i == 0)
    def _():
        win[0:CONV_HALO, :] = zeros((CONV_HALO, win.shape[1]))
        win[CONV_HALO + tm:, :] = zeros((win.shape[0] - CONV_HALO - tm, win.shape[1]))
        conv_tile(v0_ref, 0)

    @pl.when((i + 1) % tiles_per_batch == 0)
    def _():
        win[0:CONV_HALO, :] = zeros((CONV_HALO, win.shape[1]))

    conv_tile(vn_ref, (i + 1) % 2)
    ys = jnp.concatenate([ys_ref[q] for q in range(ys_ref.shape[0])], axis=-1)
    glu = ys * jax.nn.sigmoid(_dot(ys.astype(BF16), wgate_ref[...]))
    y_s5 = _dot(glu.astype(BF16), wup_ref[...])
    y_conv = _dot(cact[i % 2], wco_ref[...])
    d = y_s5.shape[1]
    merged = gs_ref[:, :d].astype(F32) * y_s5 + gs_ref[:, d:].astype(F32) * y_conv
    o_ref[...] = merged.astype(BF16)


def _branch(ys, v, gates, w_gate, w_up, w_co, w_dw, b_dw, ln_g, ln_b, seq, tm):
    n_lt, t, _ = ys.shape
    cc = v.shape[1]
    d = w_up.shape[1]
    span = w_dw.shape[0]
    n_tiles = t // tm
    full = lambda a: pl.BlockSpec(a.shape, lambda i: (0, 0))
    row = lambda a: a.reshape(1, cc)
    w_dw, b_dw, ln_g, ln_b = w_dw.reshape(span, cc), row(b_dw), row(ln_g), row(ln_b)
    kern = functools.partial(_branch_kernel, span=span, tiles_per_batch=seq // tm)
    return pl.pallas_call(
        kern,
        grid=(n_tiles,),
        in_specs=[pl.BlockSpec((n_lt, tm, LANES), lambda i: (0, i, 0)),
                  pl.BlockSpec((tm, cc), lambda i: (0, 0)),
                  pl.BlockSpec((tm, cc), lambda i: (jnp.minimum(i + 1, n_tiles - 1), 0)),
                  pl.BlockSpec((tm, 2 * d), lambda i: (i, 0)),
                  full(w_gate), full(w_up), full(w_co), full(w_dw), full(b_dw), full(ln_g), full(ln_b)],
        out_specs=pl.BlockSpec((tm, d), lambda i: (i, 0)),
        out_shape=jax.ShapeDtypeStruct((t, d), BF16),
        scratch_shapes=[pltpu.VMEM((CONV_HALO + tm + 8, cc), F32), pltpu.VMEM((2, tm, cc), BF16)],
        compiler_params=_cparams("arbitrary"),
        name="branch",
    )(ys, v, v, gates, w_gate, w_up, w_co, w_dw, b_dw, ln_g, ln_b)


def _mix_kernel(m_ref, x_ref, gate_ref, g1_ref, b1_ref, scale2_ref, shift2_ref, wout_ref, wr_ref, br_ref,
                x1_ref, h2_ref, route_ref, routet_ref, cnt_ref, run, *, alpha, n_exp, n_grp):
    @pl.when(pl.program_id(0) == 0)
    def _():
        run[...] = jnp.zeros_like(run)

    mix = _dot(m_ref[...], wout_ref[...])
    x1 = _ln(alpha * x_ref[...] + (1.0 + gate_ref[0]) * mix) * g1_ref[...] + b1_ref[...]
    x1_ref[...] = x1
    h2 = _ln(x1) * (1.0 + scale2_ref[0]) + shift2_ref[0]
    _store_rows(h2_ref, 0, h2.shape[0], h2.shape[1] // (2 * LANES), _pack_pairs(h2))

    logits = _dot(h2, wr_ref[...]) + br_ref[...]
    tm = logits.shape[0]
    per = n_exp // n_grp
    lane = lax.broadcasted_iota(jnp.int32, logits.shape, 1)
    big = jnp.int32(LANES)
    neg = jnp.float32(-jnp.inf)

    def first_max(val):
        top = jnp.max(val, axis=-1, keepdims=True)
        return top, jnp.min(jnp.where(val == top, lane, big), axis=-1, keepdims=True)

    gmask = (lane >= n_exp) & (lane < n_exp + n_grp)
    gtop, glane = first_max(jnp.where(gmask, logits, neg))
    grp = glane - n_exp
    grp_w = 1.0 / jnp.sum(jnp.where(gmask, jnp.exp(logits - gtop), 0.0), axis=-1, keepdims=True)
    emask = (lane >= grp * per) & (lane < grp * per + per)
    el = jnp.where(emask, logits, neg)
    t1, e1 = first_max(el)
    t2, e2 = first_max(jnp.where(lane == e1, neg, el))
    ex = jnp.exp(t2 - t1)
    w1 = grp_w / (1.0 + ex)
    w2 = grp_w * ex / (1.0 + ex)

    oh1 = (lane == e1).astype(F32)
    oh2 = (lane == e2).astype(F32)
    both = oh1 + oh2
    r_i = lax.broadcasted_iota(jnp.int32, (tm, tm), 0)
    c_i = lax.broadcasted_iota(jnp.int32, (tm, tm), 1)
    before = _dot((r_i > c_i).astype(F32), both) + run[...]
    rank1 = jnp.sum(before * oh1, axis=-1, keepdims=True)
    rank2 = jnp.sum(before * oh2, axis=-1, keepdims=True)
    run[...] = run[...] + jnp.sum(both, axis=0, keepdims=True)
    cnt_ref[...] = run[...]

    route = jnp.where(lane == 0, e1.astype(F32), 0.0)
    route = jnp.where(lane == 1, e2.astype(F32), route)
    route = jnp.where(lane == 2, w1, route)
    route = jnp.where(lane == 3, w2, route)
    route = jnp.where(lane == 4, rank1, route)
    route = jnp.where(lane == 5, rank2, route)
    route_ref[...] = route
    routet_ref[...] = route.T[0:routet_ref.shape[0], :]


def _mix(merged, x2, gate1, g1, b1, scale2, shift2, w_out, w_r, b_r, seq, tm, alpha, n_exp, n_grp):
    t, d = x2.shape
    k = d // (2 * LANES)
    tiles_per_batch = seq // tm
    tile = pl.BlockSpec((tm, d), lambda i: (i, 0))
    per_batch = pl.BlockSpec((1, 1, d), lambda i: (i // tiles_per_batch, 0, 0))
    const = lambda a: pl.BlockSpec(a.shape, lambda i: (0, 0))
    g1, b1 = g1.reshape(1, d), b1.reshape(1, d)
    kern = functools.partial(_mix_kernel, alpha=alpha, n_exp=n_exp, n_grp=n_grp)
    return pl.pallas_call(
        kern,
        grid=(t // tm,),
        in_specs=[tile, tile, per_batch, const(g1), const(b1), per_batch, per_batch,
                  const(w_out), const(w_r), const(b_r)],
        out_specs=[tile, pl.BlockSpec((tm * k, LANES), lambda i: (i, 0)),
                   pl.BlockSpec((tm, LANES), lambda i: (i, 0)),
                   pl.BlockSpec((8, tm), lambda i: (0, i)),
                   pl.BlockSpec((1, LANES), lambda i: (0, 0))],
        out_shape=[jax.ShapeDtypeStruct((t, d), F32), jax.ShapeDtypeStruct((t * k, LANES), jnp.uint32),
                   jax.ShapeDtypeStruct((t, LANES), F32), jax.ShapeDtypeStruct((8, t), F32),
                   jax.ShapeDtypeStruct((1, LANES), F32)],
        scratch_shapes=[pltpu.VMEM((1, LANES), F32)],
        compiler_params=_cparams("arbitrary"),
        name="mix_route",
    )(merged, x2, gate1, g1, b1, scale2, shift2, w_out, w_r, b_r)


def _moe_kernel(bexp_ref, nused_ref, nvalid_ref, tok_ref, tok_next_ref, slot_ref, h_hbm, wg_ref, wu_ref, wd_ref,
                y_hbm, xbuf, ybuf, gsem, ssem):
    del bexp_ref
    rows = tok_ref.shape[2]
    k = xbuf.shape[1] // rows
    group = MOE_DMA_GROUP
    b = pl.program_id(0)
    last = pl.num_programs(0) - 1
    n_used = nused_ref[0]
    cur, nxt = b % 2, (b + 1) % 2
    nv_cur = nvalid_ref[b]
    nv_next = nvalid_ref[jnp.minimum(b + 1, last)]

    def per_group(n_valid, fn):
        for g0 in range(0, rows, group):
            pl.when(g0 < n_valid)(functools.partial(fn, g0))

    def gather_start(idx_ref, buf, n_valid):
        def start(g0):
            for r in range(g0, g0 + group):
                src = h_hbm.at[pl.ds(pl.multiple_of(idx_ref[0, 0, r], k), k)]
                pltpu.make_async_copy(src, xbuf.at[buf, pl.ds(r * k, k)], gsem.at[buf]).start()
        per_group(n_valid, start)

    def gather_wait(buf, n_valid):
        def wait(g0):
            pltpu.make_async_copy(h_hbm.at[pl.ds(0, group * k)], xbuf.at[buf, pl.ds(g0 * k, group * k)],
                                  gsem.at[buf]).wait()
        per_group(n_valid, wait)

    def scatter_start(buf, n_valid):
        def start(g0):
            for r in range(g0, g0 + group):
                dst = y_hbm.at[pl.ds(pl.multiple_of(slot_ref[0, 0, r], k), k)]
                pltpu.make_async_copy(ybuf.at[buf, pl.ds(r * k, k)], dst, ssem.at[buf]).start()
        per_group(n_valid, start)

    def scatter_wait(buf, n_valid):
        def wait(g0):
            pltpu.make_async_copy(ybuf.at[buf, pl.ds(g0 * k, group * k)], y_hbm.at[pl.ds(0, group * k)],
                                  ssem.at[buf]).wait()
        per_group(n_valid, wait)

    @pl.when(b == 0)
    def _():
        xbuf[...] = jnp.zeros(xbuf.shape, xbuf.dtype)
        ybuf[1] = jnp.zeros(ybuf.shape[1:], ybuf.dtype)
        spare = pltpu.make_async_copy(ybuf.at[1], y_hbm.at[pl.ds(y_hbm.shape[0] - rows * k, rows * k)], ssem.at[1])
        spare.start()
        spare.wait()
        gather_start(tok_ref, 0, nv_cur)

    @pl.when((b >= 2) & (b < n_used))
    def _():
        scatter_wait(cur, nvalid_ref[jnp.maximum(b - 2, 0)])

    @pl.when(b < n_used)
    def _():
        gather_start(tok_next_ref, nxt, nv_next)
        gather_wait(cur, nv_cur)
        x = _unpack_pairs(_load_rows(xbuf.at[cur], 0, rows, k, k))
        hg = _dot(x, wg_ref[0])
        act = hg * jax.nn.sigmoid(hg) * _dot(x, wu_ref[0])
        _store_rows(ybuf.at[cur], 0, rows, k, _pack_pairs(_dot(act, wd_ref[0])))
        scatter_start(cur, nv_cur)

    @pl.when(b == n_used - 1)
    def _():
        gather_wait(nxt, nv_next)
        scatter_wait(cur, nv_cur)

        @pl.when(b >= 1)
        def _():
            scatter_wait(nxt, nvalid_ref[jnp.maximum(b - 1, 0)])


def _moe(h2, row_tok, row_slot, block_expert, n_used, n_valid, w_g, w_u, w_d, n_slots):
    de, d = w_d.shape[-2:]
    k = d // (2 * LANES)
    n_blocks = row_tok.shape[0]
    smem = lambda off: pl.BlockSpec((1, 1, MOE_ROWS),
                                    lambda b, e, n, v: (jnp.minimum(b + off, n_blocks - 1), 0, 0),
                                    memory_space=pltpu.SMEM)
    grid_spec = pltpu.PrefetchScalarGridSpec(
        num_scalar_prefetch=3,
        grid=(n_blocks,),
        in_specs=[smem(0), smem(1), smem(0),
                  pl.BlockSpec(memory_space=pl.ANY),
                  pl.BlockSpec((1, d, de), lambda b, e, n, v: (e[b], 0, 0)),
                  pl.BlockSpec((1, d, de), lambda b, e, n, v: (e[b], 0, 0)),
                  pl.BlockSpec((1, de, d), lambda b, e, n, v: (e[b], 0, 0))],
        out_specs=pl.BlockSpec(memory_space=pl.ANY),
        scratch_shapes=[pltpu.VMEM((2, MOE_ROWS * k, LANES), jnp.uint32),
                        pltpu.VMEM((2, MOE_ROWS * k, LANES), jnp.uint32),
                        pltpu.SemaphoreType.DMA((2,)), pltpu.SemaphoreType.DMA((2,))],
    )
    return pl.pallas_call(
        _moe_kernel,
        grid_spec=grid_spec,
        out_shape=jax.ShapeDtypeStruct(((n_slots + MOE_ROWS) * k, LANES), jnp.uint32),
        compiler_params=_cparams("arbitrary"),
        name="moe",
    )(block_expert, n_used, n_valid, row_tok, row_tok, row_slot, h2, w_g, w_u, w_d)


def _dispatch_tables(route_t, counts, n_exp, k):
    t = route_t.shape[1]
    n_slots = t * TOP_K
    n_blocks = n_slots // MOE_ROWS + n_exp
    slot_e = route_t[0:TOP_K].astype(jnp.int32)
    rank = route_t[4:4 + TOP_K].astype(jnp.int32)
    cnt = counts[0, :n_exp].astype(jnp.int32)
    padded = (cnt + MOE_ROWS - 1) // MOE_ROWS * MOE_ROWS
    pends = jnp.cumsum(padded)
    pstarts = pends - padded
    experts = jnp.arange(n_exp, dtype=jnp.int32)[:, None, None]
    dest = (jnp.sum(jnp.where(slot_e[None] == experts, pstarts[:, None, None], 0), axis=0) + rank).reshape(-1)
    slot_id = (TOP_K * jnp.arange(t, dtype=jnp.int32)[None, :] + jnp.arange(TOP_K, dtype=jnp.int32)[:, None])
    spare = n_slots + jnp.arange(n_blocks * MOE_ROWS, dtype=jnp.int32) % MOE_ROWS
    row_slot = spare.at[dest].set(slot_id.reshape(-1))
    row_tok = jnp.where(row_slot < n_slots, row_slot // TOP_K, 0)
    block_start = jnp.arange(n_blocks, dtype=jnp.int32) * MOE_ROWS
    block_expert = jnp.minimum(jnp.searchsorted(pends, block_start, side='right'), n_exp - 1).astype(jnp.int32)
    n_valid = jnp.clip((pstarts + cnt)[block_expert] - block_start, 0, MOE_ROWS).astype(jnp.int32)
    n_used = (pends[-1:] // MOE_ROWS).astype(jnp.int32)
    shape3 = (n_blocks, 1, MOE_ROWS)
    return (row_tok * k).reshape(shape3), (row_slot * k).reshape(shape3), block_expert, n_used, n_valid


def _final_kernel(y_ref, route_ref, x1_ref, gate_ref, g_ref, b_ref, o_ref, *, alpha):
    tm, d = x1_ref.shape
    k = d // (2 * LANES)
    route = route_ref[...]
    y0 = _unpack_pairs(_load_rows(y_ref, 0, tm, k, TOP_K * k))
    y1 = _unpack_pairs(_load_rows(y_ref, k, tm, k, TOP_K * k))
    ffn = route[:, 2:3] * y0 + route[:, 3:4] * y1
    o_ref[...] = _ln(alpha * x1_ref[...] + (1.0 + gate_ref[0]) * ffn) * g_ref[...] + b_ref[...]


def _final(y_slots, route, x1, gate2, g2, b2, seq, tm, alpha):
    t, d = x1.shape
    k = d // (2 * LANES)
    tiles_per_batch = seq // tm
    return pl.pallas_call(
        functools.partial(_final_kernel, alpha=alpha),
        grid=(t // tm,),
        in_specs=[pl.BlockSpec((tm * TOP_K * k, LANES), lambda i: (i, 0)),
                  pl.BlockSpec((tm, LANES), lambda i: (i, 0)),
                  pl.BlockSpec((tm, d), lambda i: (i, 0)),
                  pl.BlockSpec((1, 1, d), lambda i: (i // tiles_per_batch, 0, 0)),
                  pl.BlockSpec((1, d), lambda i: (0, 0)),
                  pl.BlockSpec((1, d), lambda i: (0, 0))],
        out_specs=pl.BlockSpec((tm, d), lambda i: (i, 0)),
        out_shape=jax.ShapeDtypeStruct((t, d), F32),
        compiler_params=_cparams("arbitrary"),
        name="final",
    )(y_slots, route, x1, gate2, g2.reshape(1, d), b2.reshape(1, d))


def _tiles(seq):
    return dict(proj=min(1024, seq), branch=min(256, seq), mix=min(256, seq), final=min(512, seq))


def kernel(x, c, w_ada, b_ada, w_in, b_in, s5_a_re, s5_a_im, s5_log_dt, s5_b_re, s5_b_im, s5_c_re, s5_c_im, s5_d, w_s5_gate, w_s5_up, conv_dw, conv_dw_b, conv_ln_g, conv_ln_b, w_conv_out, w_out, ln1_g, ln1_b, w_route_group, b_route_group, w_route_expert, b_route_expert, w_exp_gate, w_exp_up, w_exp_down, ln2_g, ln2_b):
    bsz, seq, d = x.shape
    t = bsz * seq
    depth = w_ada.shape[0]
    alpha = (2.0 * depth) ** 0.25
    s5_width = w_s5_gate.shape[1]
    conv_ch = conv_dw.shape[-1]
    n_grp = w_route_group.shape[-1]
    n_exp = w_route_expert.shape[-1]
    tl = _tiles(seq)
    tn = min(512, s5_width)
    x2 = x.reshape(t, d)
    for l in range(depth):
        mod = _ada(c, w_ada[l], b_ada[l])
        shift1, scale1, gate1, shift2, scale2, gate2 = [
            m.reshape(bsz, 1, d) for m in jnp.split(mod, 6, axis=-1)]

        u, v, gates = _proj(x2, shift1, scale1, w_in[l].astype(BF16), b_in[l], s5_width, conv_ch, seq,
                            tl["proj"], tn)
        tables = _s5_tables(s5_a_re[l], s5_a_im[l], s5_log_dt[l], s5_b_re[l], s5_b_im[l],
                            s5_c_re[l], s5_c_im[l], s5_d[l], S5_CHUNK)
        ys = _s5(u, tables, bsz, S5_CHUNK)
        merged = _branch(ys, v, gates, w_s5_gate[l].astype(BF16), w_s5_up[l].astype(BF16),
                         w_conv_out[l].astype(BF16), conv_dw[l], conv_dw_b[l], conv_ln_g[l], conv_ln_b[l],
                         seq, tl["branch"])

        w_r = jnp.zeros((d, LANES), F32).at[:, :n_exp].set(w_route_expert[l])
        w_r = w_r.at[:, n_exp:n_exp + n_grp].set(w_route_group[l])
        b_r = jnp.zeros((1, LANES), F32).at[0, :n_exp].set(b_route_expert[l])
        b_r = b_r.at[0, n_exp:n_exp + n_grp].set(b_route_group[l])
        x1, h2, route, route_t, counts = _mix(merged, x2, gate1, ln1_g[l], ln1_b[l], scale2, shift2,
                                              w_out[l].astype(BF16), w_r, b_r, seq, tl["mix"], alpha, n_exp, n_grp)
        row_tok, row_slot, block_expert, n_used, n_valid = _dispatch_tables(route_t, counts, n_exp,
                                                                            d // (2 * LANES))
        y_slots = _moe(h2, row_tok, row_slot, block_expert, n_used, n_valid, w_exp_gate[l], w_exp_up[l],
                       w_exp_down[l], t * TOP_K)
        x2 = _final(y_slots, route, x1, gate2, ln2_g[l], ln2_b[l], seq, tl["final"], alpha)
    return x2.reshape(bsz, seq, d)
```

```python
import functools

import jax
import jax.numpy as jnp
from jax import lax
from jax.experimental import pallas as pl
from jax.experimental.pallas import tpu as pltpu

F32 = jnp.float32
BF16 = jnp.bfloat16
LN_EPS = 1e-5
TOP_K = 2
LANES = 128
S5_CHUNK = 16
CONV_HALO = 32
MOE_ROWS = 256
MOE_DMA_GROUP = 16
VMEM_LIMIT = 56 * 1024 * 1024


def _cparams(*sem):
    return pltpu.CompilerParams(dimension_semantics=sem, vmem_limit_bytes=VMEM_LIMIT)


def _ln(x):
    mu = jnp.mean(x, axis=-1, keepdims=True)
    xc = x - mu
    var = jnp.mean(xc * xc, axis=-1, keepdims=True)
    return xc * lax.rsqrt(var + LN_EPS)


def _dot(a, b):
    return jnp.dot(a, b, preferred_element_type=F32)


def _sigmoid(x):
    return 0.5 * jnp.tanh(0.5 * x) + 0.5


def _pack_pairs(a):
    w = a.shape[1] // 2
    hi = lax.bitcast_convert_type(a[:, :w].astype(BF16).astype(F32), jnp.uint32)
    lo = lax.bitcast_convert_type(a[:, w:].astype(BF16).astype(F32), jnp.uint32)
    return hi | (lo >> 16)


def _unpack_pairs(words):
    hi = lax.bitcast_convert_type(words & jnp.uint32(0xFFFF0000), F32)
    lo = lax.bitcast_convert_type(words << 16, F32)
    return jnp.concatenate([hi, lo], axis=-1)


def _store_rows(ref, first, n, k, packed):
    for j in range(k):
        ref[pl.ds(first + j, n, stride=k), :] = packed[:, j * LANES:(j + 1) * LANES]


def _load_rows(ref, first, n, k, stride):
    return jnp.concatenate([ref[pl.ds(first + j, n, stride=stride), :] for j in range(k)], axis=-1)


def _ada_kernel(c_ref, w_ref, b_ref, o_ref):
    c = c_ref[...]
    o_ref[...] = _dot(c * jax.nn.sigmoid(c), w_ref[...]) + b_ref[...]


def _ada(c, w, b, tn=1024):
    bsz, d = c.shape
    n = w.shape[1]
    rows = 8
    cp = jnp.zeros((rows, d), F32).at[:bsz].set(c)
    out = pl.pallas_call(
        _ada_kernel,
        grid=(n // tn,),
        in_specs=[pl.BlockSpec((rows, d), lambda j: (0, 0)),
                  pl.BlockSpec((d, tn), lambda j: (0, j)),
                  pl.BlockSpec((1, tn), lambda j: (0, j))],
        out_specs=pl.BlockSpec((rows, tn), lambda j: (0, j)),
        out_shape=jax.ShapeDtypeStruct((rows, n), F32),
        compiler_params=_cparams("arbitrary"),
        name="ada",
    )(cp, w, b.reshape(1, n))
    return out[:bsz]


def _proj_kernel(x_ref, shift_ref, scale_ref, w_ref, b_ref, u_ref, v_ref, g_ref, h_scr, a_scr,
                 *, n_u, n_a):
    j = pl.program_id(1)

    @pl.when(j == 0)
    def _():
        h = _ln(x_ref[...]) * (1.0 + scale_ref[0]) + shift_ref[0]
        h_scr[...] = h.astype(BF16)

    p = _dot(h_scr[...], w_ref[...]) + b_ref[...]

    @pl.when(j < n_u)
    def _():
        for q in range(u_ref.shape[0]):
            u_ref[q] = p[:, q * LANES:(q + 1) * LANES]

    @pl.when((j >= n_u) & (j < n_u + n_a))
    def _():
        a_scr[j - n_u] = p

    @pl.when((j >= n_u + n_a) & (j < n_u + 2 * n_a))
    def _():
        v_ref[...] = a_scr[j - n_u - n_a] * _sigmoid(p)

    @pl.when(j >= n_u + 2 * n_a)
    def _():
        g_ref[...] = _sigmoid(p.astype(BF16))


def _proj(x2, shift, scale, w_bf, b, s5_width, conv_ch, seq, tm, tn):
    t, d = x2.shape
    n = w_bf.shape[1]
    n_u, n_a = s5_width // tn, conv_ch // tn
    assert n_u == 1, "the S5 input must be one column tile"
    n_lt = s5_width // LANES
    n_g = (n - s5_width - 2 * conv_ch) // tn
    tiles_per_batch = seq // tm
    g0 = n_u + 2 * n_a
    kern = functools.partial(_proj_kernel, n_u=n_u, n_a=n_a)
    return pl.pallas_call(
        kern,
        grid=(t // tm, n // tn),
        in_specs=[pl.BlockSpec((tm, d), lambda i, j: (i, 0)),
                  pl.BlockSpec((1, 1, d), lambda i, j: (i // tiles_per_batch, 0, 0)),
                  pl.BlockSpec((1, 1, d), lambda i, j: (i // tiles_per_batch, 0, 0)),
                  pl.BlockSpec((d, tn), lambda i, j: (0, j)),
                  pl.BlockSpec((1, tn), lambda i, j: (0, j))],
        out_specs=[pl.BlockSpec((n_lt, tm, LANES), lambda i, j: (0, i, 0)),
                   pl.BlockSpec((tm, tn), lambda i, j: (i, jnp.clip(j - n_u - n_a, 0, n_a - 1))),
                   pl.BlockSpec((tm, tn), lambda i, j: (i, jnp.clip(j - g0, 0, n_g - 1)))],
        out_shape=[jax.ShapeDtypeStruct((n_lt, t, LANES), F32),
                   jax.ShapeDtypeStruct((t, conv_ch), F32),
                   jax.ShapeDtypeStruct((t, n_g * tn), BF16)],
        scratch_shapes=[pltpu.VMEM((tm, d), BF16), pltpu.VMEM((n_a, tm, tn), F32)],
        compiler_params=_cparams("arbitrary", "arbitrary"),
        name="proj",
    )(x2, shift, scale, w_bf, b.reshape(1, n))


def _s5_tables(a_re, a_im, log_dt, b_re, b_im, c_re, c_im, d_skip, chunk):
    hp = lax.Precision.HIGHEST
    g, p = a_re.shape
    hw = b_re.shape[-1]
    w = chunk * hw
    dt = jnp.exp(log_dt.astype(F32))[:, None, None]
    lr, li = a_re.astype(F32)[:, :, None], a_im.astype(F32)[:, :, None]

    def power(k):
        mag = jnp.exp(k * lr * dt)
        return mag * jnp.cos(k * li * dt), mag * jnp.sin(k * li * dt)

    a1_re, a1_im = power(1.0)
    den = lr * lr + li * li
    nr, ni = a1_re - 1.0, a1_im
    z_re = (nr * lr + ni * li) / den
    z_im = (ni * lr - nr * li) / den
    br, bi = b_re.astype(F32), b_im.astype(F32)
    bb_re = z_re * br - z_im * bi
    bb_im = z_re * bi + z_im * br
    lag = (jnp.arange(w + hw) // hw).astype(F32)[None, None, :]
    pw_re, pw_im = power(lag)
    ct_re = jnp.tile(c_re.astype(F32).transpose(0, 2, 1), (1, 1, chunk + 1))
    ct_im = jnp.tile(c_im.astype(F32).transpose(0, 2, 1), (1, 1, chunk + 1))
    ca_re = ct_re * pw_re - ct_im * pw_im
    ca_im = ct_re * pw_im + ct_im * pw_re
    kt = (jnp.einsum('gpj,gpl->gjl', bb_re, ca_re[:, :, :w], precision=hp)
          - jnp.einsum('gpj,gpl->gjl', bb_im, ca_im[:, :, :w], precision=hp))
    m = jnp.stack([jnp.pad(kt[:, :, :w - s * hw], ((0, 0), (0, 0), (s * hw, 0))) for s in range(chunk)],
                  axis=1).reshape(g, w, w)
    rev = (chunk - 1 - jnp.arange(w) // hw).astype(F32)[None, None, :]
    rv_re, rv_im = power(rev)
    bt_re, bt_im = jnp.tile(bb_re, (1, 1, chunk)), jnp.tile(bb_im, (1, 1, chunk))
    pt_re = rv_re * bt_re - rv_im * bt_im
    pt_im = rv_re * bt_im + rv_im * bt_re
    qo_re, qo_im = ca_re[:, :, hw:], -ca_im[:, :, hw:]
    al_re, al_im = power(float(chunk))
    al_re, al_im = al_re.reshape(g, 1, p), al_im.reshape(g, 1, p)
    d_t = jnp.tile(d_skip.astype(F32)[:, None, :], (1, 1, chunk))
    return m, pt_re, pt_im, qo_re, qo_im, al_re, al_im, d_t


def _s5_kernel(u_ref, m_ref, pre_ref, pim_ref, qre_ref, qim_ref, are_ref, aim_ref, d_ref, y_ref,
               zre, zim, sre, sim, *, bsz):
    u = u_ref[0]
    rows = u.shape[0]
    nc = rows // bsz
    y = _dot(u, m_ref[0]) + d_ref[0] * u
    nt = (((1,), (1,)), ((), ()))
    zre[...] = lax.dot_general(u, pre_ref[0], nt, preferred_element_type=F32)
    zim[...] = lax.dot_general(u, pim_ref[0], nt, preferred_element_type=F32)
    ar, ai = are_ref[0], aim_ref[0]
    npair = zre.shape[1]

    def step(c, carry):
        nxt = []
        for b in range(bsz):
            sr, si = carry[2 * b], carry[2 * b + 1]
            row = pl.ds(b * nc + c, 1)
            sre[row, :] = sr
            sim[row, :] = si
            nxt.append(ar * sr - ai * si + zre[row, :])
            nxt.append(ar * si + ai * sr + zim[row, :])
        return tuple(nxt)

    zero = jnp.zeros((1, npair), F32)
    lax.fori_loop(0, nc, step, tuple(zero for _ in range(2 * bsz)))
    y = y + _dot(sre[...], qre_ref[0]) + _dot(sim[...], qim_ref[0])
    y_ref[0] = jax.nn.gelu(y, approximate=True)


def _pack_kernel(u_ref, o_ref, *, chunk):
    g, cb, _ = o_ref.shape
    n_lt = u_ref.shape[0]
    per = g // n_lt
    hw = LANES // per
    for s in range(chunk):
        for q in range(n_lt):
            rows = u_ref[q, pl.ds(s, cb, stride=chunk), :]
            for i in range(per):
                o_ref[q * per + i, :, s * hw:(s + 1) * hw] = rows[:, i * hw:(i + 1) * hw]


def _unpack_kernel(y_ref, o_ref, tmp, *, chunk):
    g, cb, _ = y_ref.shape
    n_lt = o_ref.shape[0]
    per = g // n_lt
    hw = LANES // per
    for s in range(chunk):
        for q in range(n_lt):
            for i in range(per):
                tmp[:, i * hw:(i + 1) * hw] = y_ref[q * per + i, :, s * hw:(s + 1) * hw]
            o_ref[q, pl.ds(s, cb, stride=chunk), :] = tmp[...]


def _s5(u, tables, bsz, chunk):
    m, p_re, p_im, q_re, q_im, a_re, a_im, d_t = tables
    n_lt, t, _ = u.shape
    g, cw, _ = m.shape
    npair = p_re.shape[1]
    rows = t // chunk
    cb = min(64, rows)
    by_chunk = pl.BlockSpec((n_lt, cb * chunk, LANES), lambda i: (0, i, 0))
    by_group = pl.BlockSpec((g, cb, cw), lambda i: (0, i, 0))
    ut = pl.pallas_call(
        functools.partial(_pack_kernel, chunk=chunk), grid=(rows // cb,), in_specs=[by_chunk],
        out_specs=by_group, out_shape=jax.ShapeDtypeStruct((g, rows, cw), F32),
        compiler_params=_cparams("arbitrary"), name="s5_pack",
    )(u)
    gspec = lambda *shape: pl.BlockSpec((1,) + shape, lambda i: (i, 0, 0))
    yt = pl.pallas_call(
        functools.partial(_s5_kernel, bsz=bsz),
        grid=(g,),
        in_specs=[gspec(rows, cw), gspec(cw, cw), gspec(npair, cw), gspec(npair, cw),
                  gspec(npair, cw), gspec(npair, cw), gspec(1, npair), gspec(1, npair), gspec(1, cw)],
        out_specs=gspec(rows, cw),
        out_shape=jax.ShapeDtypeStruct((g, rows, cw), F32),
        scratch_shapes=[pltpu.VMEM((rows, npair), F32) for _ in range(4)],
        compiler_params=_cparams("arbitrary"),
        name="s5",
    )(ut, m, p_re, p_im, q_re, q_im, a_re, a_im, d_t)
    return pl.pallas_call(
        functools.partial(_unpack_kernel, chunk=chunk), grid=(rows // cb,), in_specs=[by_group],
        out_specs=by_chunk, out_shape=jax.ShapeDtypeStruct((n_lt, t, LANES), F32),
        scratch_shapes=[pltpu.VMEM((cb, LANES), F32)],
        compiler_params=_cparams("arbitrary"), name="s5_unpack",
    )(yt)


def _conv_window(win, w_ref, b_ref, g_ref, beta_ref, ts, span):
    sub = 8
    base = CONV_HALO - (span - 1)
    acc = None
    for q in range(sub):
        part = None
        for k in range(span):
            if (base + k) % sub == q:
                lo = base + k - q
                term = win[lo:lo + ts + sub, :] * w_ref[k:k + 1, :]
                part = term if part is None else part + term
        if part is not None:
            part = part[q:q + ts, :]
            acc = part if acc is None else acc + part
    y = _ln(acc + b_ref[...]) * g_ref[...] + beta_ref[...]
    return y * jax.nn.sigmoid(y)


def _branch_kernel(ys_ref, v0_ref, vn_ref, gs_ref, wgate_ref, wup_ref, wco_ref, wdw_ref, bdw_ref, lng_ref,
                   lnb_ref, o_ref, win, cact, *, span, tiles_per_batch):
    i = pl.program_id(0)
    tm = o_ref.shape[0]
    zeros = functools.partial(jnp.zeros, dtype=F32)

    def conv_tile(v_ref, slot):
        win[CONV_HALO:CONV_HALO + tm, :] = v_ref[...]
        act = _conv_window(win, wdw_ref, bdw_ref, lng_ref, lnb_ref, tm, span)
        cact[slot] = act.astype(BF16)
        win[0:CONV_HALO, :] = win[tm:tm + CONV_HALO, :]

    @pl.when(i == 0)
    def _():
        win[0:CONV_HALO, :] = zeros((CONV_HALO, win.shape[1]))
        win[CONV_HALO + tm:, :] = zeros((win.shape[0] - CONV_HALO - tm, win.shape[1]))
        conv_tile(v0_ref, 0)

    @pl.when((i + 1) % tiles_per_batch == 0)
    def _():
        win[0:CONV_HALO, :] = zeros((CONV_HALO, win.shape[1]))

    conv_tile(vn_ref, (i + 1) % 2)
    ys = jnp.concatenate([ys_ref[q] for q in range(ys_ref.shape[0])], axis=-1)
    glu = ys * jax.nn.sigmoid(_dot(ys.astype(BF16), wgate_ref[...]))
    y_s5 = _dot(glu.astype(BF16), wup_ref[...])
    y_conv = _dot(cact[i % 2], wco_ref[...])
    d = y_s5.shape[1]
    merged = gs_ref[:, :d].astype(F32) * y_s5 + gs_ref[:, d:].astype(F32) * y_conv
    o_ref[...] = merged.astype(BF16)


def _branch(ys, v, gates, w_gate, w_up, w_co, w_dw, b_dw, ln_g, ln_b, seq, tm):
    n_lt, t, _ = ys.shape
    cc = v.shape[1]
    d = w_up.shape[1]
    span = w_dw.shape[0]
    n_tiles = t // tm
    full = lambda a: pl.BlockSpec(a.shape, lambda i: (0, 0))
    row = lambda a: a.reshape(1, cc)
    w_dw, b_dw, ln_g, ln_b = w_dw.reshape(span, cc), row(b_dw), row(ln_g), row(ln_b)
    kern = functools.partial(_branch_kernel, span=span, tiles_per_batch=seq // tm)
    return pl.pallas_call(
        kern,
        grid=(n_tiles,),
        in_specs=[pl.BlockSpec((n_lt, tm, LANES), lambda i: (0, i, 0)),
                  pl.BlockSpec((tm, cc), lambda i: (0, 0)),
                  pl.BlockSpec((tm, cc), lambda i: (jnp.minimum(i + 1, n_tiles - 1), 0)),
                  pl.BlockSpec((tm, 2 * d), lambda i: (i, 0)),
                  full(w_gate), full(w_up), full(w_co), full(w_dw), full(b_dw), full(ln_g), full(ln_b)],
        out_specs=pl.BlockSpec((tm, d), lambda i: (i, 0)),
        out_shape=jax.ShapeDtypeStruct((t, d), BF16),
        scratch_shapes=[pltpu.VMEM((CONV_HALO + tm + 8, cc), F32), pltpu.VMEM((2, tm, cc), BF16)],
        compiler_params=_cparams("arbitrary"),
        name="branch",
    )(ys, v, v, gates, w_gate, w_up, w_co, w_dw, b_dw, ln_g, ln_b)


def _mix_kernel(m_ref, x_ref, gate_ref, g1_ref, b1_ref, scale2_ref, shift2_ref, wout_ref, wr_ref, br_ref,
                x1_ref, h2_ref, route_ref, routet_ref, cnt_ref, run, *, alpha, n_exp, n_grp):
    @pl.when(pl.program_id(0) == 0)
    def _():
        run[...] = jnp.zeros_like(run)

    mix = _dot(m_ref[...], wout_ref[...])
    x1 = _ln(alpha * x_ref[...] + (1.0 + gate_ref[0]) * mix) * g1_ref[...] + b1_ref[...]
    x1_ref[...] = x1
    h2 = _ln(x1) * (1.0 + scale2_ref[0]) + shift2_ref[0]
    _store_rows(h2_ref, 0, h2.shape[0], h2.shape[1] // (2 * LANES), _pack_pairs(h2))

    logits = _dot(h2, wr_ref[...]) + br_ref[...]
    tm = logits.shape[0]
    per = n_exp // n_grp
    lane = lax.broadcasted_iota(jnp.int32, logits.shape, 1)
    big = jnp.int32(LANES)
    neg = jnp.float32(-jnp.inf)

    def first_max(val):
        top = jnp.max(val, axis=-1, keepdims=True)
        return top, jnp.min(jnp.where(val == top, lane, big), axis=-1, keepdims=True)

    gmask = (lane >= n_exp) & (lane < n_exp + n_grp)
    gtop, glane = first_max(jnp.where(gmask, logits, neg))
    grp = glane - n_exp
    grp_w = 1.0 / jnp.sum(jnp.where(gmask, jnp.exp(logits - gtop), 0.0), axis=-1, keepdims=True)
    emask = (lane >= grp * per) & (lane < grp * per + per)
    el = jnp.where(emask, logits, neg)
    t1, e1 = first_max(el)
    t2, e2 = first_max(jnp.where(lane == e1, neg, el))
    ex = jnp.exp(t2 - t1)
    w1 = grp_w / (1.0 + ex)
    w2 = grp_w * ex / (1.0 + ex)

    oh1 = (lane == e1).astype(F32)
    oh2 = (lane == e2).astype(F32)
    both = oh1 + oh2
    r_i = lax.broadcasted_iota(jnp.int32, (tm, tm), 0)
    c_i = lax.broadcasted_iota(jnp.int32, (tm, tm), 1)
    before = _dot((r_i > c_i).astype(F32), both) + run[...]
    rank1 = jnp.sum(before * oh1, axis=-1, keepdims=True)
    rank2 = jnp.sum(before * oh2, axis=-1, keepdims=True)
    run[...] = run[...] + jnp.sum(both, axis=0, keepdims=True)
    cnt_ref[...] = run[...]

    route = jnp.where(lane == 0, e1.astype(F32), 0.0)
    route = jnp.where(lane == 1, e2.astype(F32), route)
    route = jnp.where(lane == 2, w1, route)
    route = jnp.where(lane == 3, w2, route)
    route = jnp.where(lane == 4, rank1, route)
    route = jnp.where(lane == 5, rank2, route)
    route_ref[...] = route
    routet_ref[...] = route.T[0:routet_ref.shape[0], :]


def _mix(merged, x2, gate1, g1, b1, scale2, shift2, w_out, w_r, b_r, seq, tm, alpha, n_exp, n_grp):
    t, d = x2.shape
    k = d // (2 * LANES)
    tiles_per_batch = seq // tm
    tile = pl.BlockSpec((tm, d), lambda i: (i, 0))
    per_batch = pl.BlockSpec((1, 1, d), lambda i: (i // tiles_per_batch, 0, 0))
    const = lambda a: pl.BlockSpec(a.shape, lambda i: (0, 0))
    g1, b1 = g1.reshape(1, d), b1.reshape(1, d)
    kern = functools.partial(_mix_kernel, alpha=alpha, n_exp=n_exp, n_grp=n_grp)
    return pl.pallas_call(
        kern,
        grid=(t // tm,),
        in_specs=[tile, tile, per_batch, const(g1), const(b1), per_batch, per_batch,
                  const(w_out), const(w_r), const(b_r)],
        out_specs=[tile, pl.BlockSpec((tm * k, LANES), lambda i: (i, 0)),
                   pl.BlockSpec((tm, LANES), lambda i: (i, 0)),
                   pl.BlockSpec((8, tm), lambda i: (0, i)),
                   pl.BlockSpec((1, LANES), lambda i: (0, 0))],
        out_shape=[jax.ShapeDtypeStruct((t, d), F32), jax.ShapeDtypeStruct((t * k, LANES), jnp.uint32),
                   jax.ShapeDtypeStruct((t, LANES), F32), jax.ShapeDtypeStruct((8, t), F32),
                   jax.ShapeDtypeStruct((1, LANES), F32)],
        scratch_shapes=[pltpu.VMEM((1, LANES), F32)],
        compiler_params=_cparams("arbitrary"),
        name="mix_route",
    )(merged, x2, gate1, g1, b1, scale2, shift2, w_out, w_r, b_r)


def _moe_kernel(bexp_ref, nused_ref, nvalid_ref, tok_ref, tok_next_ref, slot_ref, h_hbm, wg_ref, wu_ref, wd_ref,
                y_hbm, xbuf, ybuf, gsem, ssem):
    del bexp_ref
    rows = tok_ref.shape[2]
    k = xbuf.shape[1] // rows
    group = MOE_DMA_GROUP
    b = pl.program_id(0)
    last = pl.num_programs(0) - 1
    n_used = nused_ref[0]
    cur, nxt = b % 2, (b + 1) % 2
    nv_cur = nvalid_ref[b]
    nv_next = nvalid_ref[jnp.minimum(b + 1, last)]

    def per_group(n_valid, fn):
        for g0 in range(0, rows, group):
            pl.when(g0 < n_valid)(functools.partial(fn, g0))

    def gather_start(idx_ref, buf, n_valid):
        def start(g0):
            for r in range(g0, g0 + group):
                src = h_hbm.at[pl.ds(pl.multiple_of(idx_ref[0, 0, r], k), k)]
                pltpu.make_async_copy(src, xbuf.at[buf, pl.ds(r * k, k)], gsem.at[buf]).start()
        per_group(n_valid, start)

    def gather_wait(buf, n_valid):
        def wait(g0):
            pltpu.make_async_copy(h_hbm.at[pl.ds(0, group * k)], xbuf.at[buf, pl.ds(g0 * k, group * k)],
                                  gsem.at[buf]).wait()
        per_group(n_valid, wait)

    def scatter_start(buf, n_valid):
        def start(g0):
            for r in range(g0, g0 + group):
                dst = y_hbm.at[pl.ds(pl.multiple_of(slot_ref[0, 0, r], k), k)]
                pltpu.make_async_copy(ybuf.at[buf, pl.ds(r * k, k)], dst, ssem.at[buf]).start()
        per_group(n_valid, start)

    def scatter_wait(buf, n_valid):
        def wait(g0):
            pltpu.make_async_copy(ybuf.at[buf, pl.ds(g0 * k, group * k)], y_hbm.at[pl.ds(0, group * k)],
                                  ssem.at[buf]).wait()
        per_group(n_valid, wait)

    @pl.when(b == 0)
    def _():
        xbuf[...] = jnp.zeros(xbuf.shape, xbuf.dtype)
        ybuf[1] = jnp.zeros(ybuf.shape[1:], ybuf.dtype)
        spare = pltpu.make_async_copy(ybuf.at[1], y_hbm.at[pl.ds(y_hbm.shape[0] - rows * k, rows * k)], ssem.at[1])
        spare.start()
        spare.wait()
        gather_start(tok_ref, 0, nv_cur)

    @pl.when((b >= 2) & (b < n_used))
    def _():
        scatter_wait(cur, nvalid_ref[jnp.maximum(b - 2, 0)])

    @pl.when(b < n_used)
    def _():
        gather_start(tok_next_ref, nxt, nv_next)
        gather_wait(cur, nv_cur)
        x = _unpack_pairs(_load_rows(xbuf.at[cur], 0, rows, k, k))
        hg = _dot(x, wg_ref[0])
        act = hg * jax.nn.sigmoid(hg) * _dot(x, wu_ref[0])
        _store_rows(ybuf.at[cur], 0, rows, k, _pack_pairs(_dot(act, wd_ref[0])))
        scatter_start(cur, nv_cur)

    @pl.when(b == n_used - 1)
    def _():
        gather_wait(nxt, nv_next)
        scatter_wait(cur, nv_cur)

        @pl.when(b >= 1)
        def _():
            scatter_wait(nxt, nvalid_ref[jnp.maximum(b - 1, 0)])


def _moe(h2, row_tok, row_slot, block_expert, n_used, n_valid, w_g, w_u, w_d, n_slots):
    de, d = w_d.shape[-2:]
    k = d // (2 * LANES)
    n_blocks = row_tok.shape[0]
    smem = lambda off: pl.BlockSpec((1, 1, MOE_ROWS),
                                    lambda b, e, n, v: (jnp.minimum(b + off, n_blocks - 1), 0, 0),
                                    memory_space=pltpu.SMEM)
    grid_spec = pltpu.PrefetchScalarGridSpec(
        num_scalar_prefetch=3,
        grid=(n_blocks,),
        in_specs=[smem(0), smem(1), smem(0),
                  pl.BlockSpec(memory_space=pl.ANY),
                  pl.BlockSpec((1, d, de), lambda b, e, n, v: (e[b], 0, 0)),
                  pl.BlockSpec((1, d, de), lambda b, e, n, v: (e[b], 0, 0)),
                  pl.BlockSpec((1, de, d), lambda b, e, n, v: (e[b], 0, 0))],
        out_specs=pl.BlockSpec(memory_space=pl.ANY),
        scratch_shapes=[pltpu.VMEM((2, MOE_ROWS * k, LANES), jnp.uint32),
                        pltpu.VMEM((2, MOE_ROWS * k, LANES), jnp.uint32),
                        pltpu.SemaphoreType.DMA((2,)), pltpu.SemaphoreType.DMA((2,))],
    )
    return pl.pallas_call(
        _moe_kernel,
        grid_spec=grid_spec,
        out_shape=jax.ShapeDtypeStruct(((n_slots + MOE_ROWS) * k, LANES), jnp.uint32),
        compiler_params=_cparams("arbitrary"),
        name="moe",
    )(block_expert, n_used, n_valid, row_tok, row_tok, row_slot, h2, w_g, w_u, w_d)


def _dispatch_tables(route_t, counts, n_exp, k):
    t = route_t.shape[1]
    n_slots = t * TOP_K
    n_blocks = n_slots // MOE_ROWS + n_exp
    slot_e = route_t[0:TOP_K].astype(jnp.int32)
    rank = route_t[4:4 + TOP_K].astype(jnp.int32)
    cnt = counts[0, :n_exp].astype(jnp.int32)
    padded = (cnt + MOE_ROWS - 1) // MOE_ROWS * MOE_ROWS
    pends = jnp.cumsum(padded)
    pstarts = pends - padded
    experts = jnp.arange(n_exp, dtype=jnp.int32)[:, None, None]
    dest = (jnp.sum(jnp.where(slot_e[None] == experts, pstarts[:, None, None], 0), axis=0) + rank).reshape(-1)
    slot_id = (TOP_K * jnp.arange(t, dtype=jnp.int32)[None, :] + jnp.arange(TOP_K, dtype=jnp.int32)[:, None])
    spare = n_slots + jnp.arange(n_blocks * MOE_ROWS, dtype=jnp.int32) % MOE_ROWS
    row_slot = spare.at[dest].set(slot_id.reshape(-1))
    row_tok = jnp.where(row_slot < n_slots, row_slot // TOP_K, 0)
    block_start = jnp.arange(n_blocks, dtype=jnp.int32) * MOE_ROWS
    block_expert = jnp.minimum(jnp.sum(block_start[None, :] >= pends[:, None], axis=0), n_exp - 1).astype(jnp.int32)
    valid_end = jnp.sum(jnp.where(block_expert[None, :] == experts[:, :, 0], (pstarts + cnt)[:, None], 0), axis=0)
    n_valid = jnp.clip(valid_end - block_start, 0, MOE_ROWS).astype(jnp.int32)
    n_used = (pends[-1:] // MOE_ROWS).astype(jnp.int32)
    shape3 = (n_blocks, 1, MOE_ROWS)
    return (row_tok * k).reshape(shape3), (row_slot * k).reshape(shape3), block_expert, n_used, n_valid


def _final_kernel(y_ref, route_ref, x1_ref, gate_ref, g_ref, b_ref, o_ref, *, alpha):
    tm, d = x1_ref.shape
    k = d // (2 * LANES)
    route = route_ref[...]
    y0 = _unpack_pairs(_load_rows(y_ref, 0, tm, k, TOP_K * k))
    y1 = _unpack_pairs(_load_rows(y_ref, k, tm, k, TOP_K * k))
    ffn = route[:, 2:3] * y0 + route[:, 3:4] * y1
    o_ref[...] = _ln(alpha * x1_ref[...] + (1.0 + gate_ref[0]) * ffn) * g_ref[...] + b_ref[...]


def _final(y_slots, route, x1, gate2, g2, b2, seq, tm, alpha):
    t, d = x1.shape
    k = d // (2 * LANES)
    tiles_per_batch = seq // tm
    return pl.pallas_call(
        functools.partial(_final_kernel, alpha=alpha),
        grid=(t // tm,),
        in_specs=[pl.BlockSpec((tm * TOP_K * k, LANES), lambda i: (i, 0)),
                  pl.BlockSpec((tm, LANES), lambda i: (i, 0)),
                  pl.BlockSpec((tm, d), lambda i: (i, 0)),
                  pl.BlockSpec((1, 1, d), lambda i: (i // tiles_per_batch, 0, 0)),
                  pl.BlockSpec((1, d), lambda i: (0, 0)),
                  pl.BlockSpec((1, d), lambda i: (0, 0))],
        out_specs=pl.BlockSpec((tm, d), lambda i: (i, 0)),
        out_shape=jax.ShapeDtypeStruct((t, d), F32),
        compiler_params=_cparams("arbitrary"),
        name="final",
    )(y_slots, route, x1, gate2, g2.reshape(1, d), b2.reshape(1, d))


def _tiles(seq):
    return dict(proj=min(1024, seq), branch=min(256, seq), mix=min(512, seq), final=min(512, seq))


def kernel(x, c, w_ada, b_ada, w_in, b_in, s5_a_re, s5_a_im, s5_log_dt, s5_b_re, s5_b_im, s5_c_re, s5_c_im, s5_d, w_s5_gate, w_s5_up, conv_dw, conv_dw_b, conv_ln_g, conv_ln_b, w_conv_out, w_out, ln1_g, ln1_b, w_route_group, b_route_group, w_route_expert, b_route_expert, w_exp_gate, w_exp_up, w_exp_down, ln2_g, ln2_b):
    bsz, seq, d = x.shape
    t = bsz * seq
    depth = w_ada.shape[0]
    alpha = (2.0 * depth) ** 0.25
    s5_width = w_s5_gate.shape[1]
    conv_ch = conv_dw.shape[-1]
    n_grp = w_route_group.shape[-1]
    n_exp = w_route_expert.shape[-1]
    tl = _tiles(seq)
    tn = min(512, s5_width)
    x2 = x.reshape(t, d)
    for l in range(depth):
        mod = _ada(c, w_ada[l], b_ada[l])
        shift1, scale1, gate1, shift2, scale2, gate2 = [
            m.reshape(bsz, 1, d) for m in jnp.split(mod, 6, axis=-1)]

        u, v, gates = _proj(x2, shift1, scale1, w_in[l].astype(BF16), b_in[l], s5_width, conv_ch, seq,
                            tl["proj"], tn)
        tables = _s5_tables(s5_a_re[l], s5_a_im[l], s5_log_dt[l], s5_b_re[l], s5_b_im[l],
                            s5_c_re[l], s5_c_im[l], s5_d[l], S5_CHUNK)
        ys = _s5(u, tables, bsz, S5_CHUNK)
        merged = _branch(ys, v, gates, w_s5_gate[l].astype(BF16), w_s5_up[l].astype(BF16),
                         w_conv_out[l].astype(BF16), conv_dw[l], conv_dw_b[l], conv_ln_g[l], conv_ln_b[l],
                         seq, tl["branch"])

        w_r = jnp.zeros((d, LANES), F32).at[:, :n_exp].set(w_route_expert[l])
        w_r = w_r.at[:, n_exp:n_exp + n_grp].set(w_route_group[l])
        b_r = jnp.zeros((1, LANES), F32).at[0, :n_exp].set(b_route_expert[l])
        b_r = b_r.at[0, n_exp:n_exp + n_grp].set(b_route_group[l])
        x1, h2, route, route_t, counts = _mix(merged, x2, gate1, ln1_g[l], ln1_b[l], scale2, shift2,
                                              w_out[l].astype(BF16), w_r, b_r, seq, tl["mix"], alpha, n_exp, n_grp)
        row_tok, row_slot, block_expert, n_used, n_valid = _dispatch_tables(route_t, counts, n_exp,
                                                                            d // (2 * LANES))
        y_slots = _moe(h2, row_tok, row_slot, block_expert, n_used, n_valid, w_exp_gate[l], w_exp_up[l],
                       w_exp_down[l], t * TOP_K)
        x2 = _final(y_slots, route, x1, gate2, ln2_g[l], ln2_b[l], seq, tl["final"], alpha)
    return x2.reshape(bsz, seq, d)
```

```python
import functools

import jax
import jax.numpy as jnp
from jax import lax
from jax.experimental import pallas as pl
from jax.experimental.pallas import tpu as pltpu

F32 = jnp.float32
BF16 = jnp.bfloat16
LN_EPS = 1e-5
TOP_K = 2
LANES = 128
S5_CHUNK = 16
CONV_HALO = 32
MOE_ROWS = 256
MOE_DMA_GROUP = 32
VMEM_LIMIT = 56 * 1024 * 1024


def _cparams(*sem):
    return pltpu.CompilerParams(dimension_semantics=sem, vmem_limit_bytes=VMEM_LIMIT)


def _ln(x):
    mu = jnp.mean(x, axis=-1, keepdims=True)
    xc = x - mu
    var = jnp.mean(xc * xc, axis=-1, keepdims=True)
    return xc * lax.rsqrt(var + LN_EPS)


def _dot(a, b):
    return jnp.dot(a, b, preferred_element_type=F32)


def _sigmoid(x):
    return 0.5 * jnp.tanh(0.5 * x) + 0.5


def _pack_pairs(a):
    w = a.shape[1] // 2
    hi = lax.bitcast_convert_type(a[:, :w].astype(BF16).astype(F32), jnp.uint32)
    lo = lax.bitcast_convert_type(a[:, w:].astype(BF16).astype(F32), jnp.uint32)
    return hi | (lo >> 16)


def _unpack_pairs(words):
    hi = lax.bitcast_convert_type(words & jnp.uint32(0xFFFF0000), F32)
    lo = lax.bitcast_convert_type(words << 16, F32)
    return jnp.concatenate([hi, lo], axis=-1)


def _store_rows(ref, first, n, k, packed):
    for j in range(k):
        ref[pl.ds(first + j, n, stride=k), :] = packed[:, j * LANES:(j + 1) * LANES]


def _load_rows(ref, first, n, k, stride):
    return jnp.concatenate([ref[pl.ds(first + j, n, stride=stride), :] for j in range(k)], axis=-1)


def _ada_kernel(c_ref, w_ref, b_ref, o_ref):
    c = c_ref[...]
    o_ref[...] = _dot(c * jax.nn.sigmoid(c), w_ref[...]) + b_ref[...]


def _ada(c, w, b, tn=1024):
    bsz, d = c.shape
    n = w.shape[1]
    rows = 8
    cp = jnp.zeros((rows, d), F32).at[:bsz].set(c)
    out = pl.pallas_call(
        _ada_kernel,
        grid=(n // tn,),
        in_specs=[pl.BlockSpec((rows, d), lambda j: (0, 0)),
                  pl.BlockSpec((d, tn), lambda j: (0, j)),
                  pl.BlockSpec((1, tn), lambda j: (0, j))],
        out_specs=pl.BlockSpec((rows, tn), lambda j: (0, j)),
        out_shape=jax.ShapeDtypeStruct((rows, n), F32),
        compiler_params=_cparams("arbitrary"),
        name="ada",
    )(cp, w, b.reshape(1, n))
    return out[:bsz]


def _proj_kernel(x_ref, shift_ref, scale_ref, w_ref, b_ref, u_ref, v_ref, g_ref, h_scr, a_scr,
                 *, n_u, n_a):
    j = pl.program_id(1)

    @pl.when(j == 0)
    def _():
        h = _ln(x_ref[...]) * (1.0 + scale_ref[0]) + shift_ref[0]
        h_scr[...] = h.astype(BF16)

    p = _dot(h_scr[...], w_ref[...]) + b_ref[...]

    @pl.when(j < n_u)
    def _():
        for q in range(u_ref.shape[0]):
            u_ref[q] = p[:, q * LANES:(q + 1) * LANES]

    @pl.when((j >= n_u) & (j < n_u + n_a))
    def _():
        a_scr[j - n_u] = p

    @pl.when((j >= n_u + n_a) & (j < n_u + 2 * n_a))
    def _():
        v_ref[...] = a_scr[j - n_u - n_a] * _sigmoid(p)

    @pl.when(j >= n_u + 2 * n_a)
    def _():
        g_ref[...] = _sigmoid(p.astype(BF16))


def _proj(x2, shift, scale, w_bf, b, s5_width, conv_ch, seq, tm, tn):
    t, d = x2.shape
    n = w_bf.shape[1]
    n_u, n_a = s5_width // tn, conv_ch // tn
    assert n_u == 1, "the S5 input must be one column tile"
    n_lt = s5_width // LANES
    n_g = (n - s5_width - 2 * conv_ch) // tn
    tiles_per_batch = seq // tm
    g0 = n_u + 2 * n_a
    kern = functools.partial(_proj_kernel, n_u=n_u, n_a=n_a)
    return pl.pallas_call(
        kern,
        grid=(t // tm, n // tn),
        in_specs=[pl.BlockSpec((tm, d), lambda i, j: (i, 0)),
                  pl.BlockSpec((1, 1, d), lambda i, j: (i // tiles_per_batch, 0, 0)),
                  pl.BlockSpec((1, 1, d), lambda i, j: (i // tiles_per_batch, 0, 0)),
                  pl.BlockSpec((d, tn), lambda i, j: (0, j)),
                  pl.BlockSpec((1, tn), lambda i, j: (0, j))],
        out_specs=[pl.BlockSpec((n_lt, tm, LANES), lambda i, j: (0, i, 0)),
                   pl.BlockSpec((tm, tn), lambda i, j: (i, jnp.clip(j - n_u - n_a, 0, n_a - 1))),
                   pl.BlockSpec((tm, tn), lambda i, j: (i, jnp.clip(j - g0, 0, n_g - 1)))],
        out_shape=[jax.ShapeDtypeStruct((n_lt, t, LANES), F32),
                   jax.ShapeDtypeStruct((t, conv_ch), F32),
                   jax.ShapeDtypeStruct((t, n_g * tn), BF16)],
        scratch_shapes=[pltpu.VMEM((tm, d), BF16), pltpu.VMEM((n_a, tm, tn), F32)],
        compiler_params=_cparams("arbitrary", "arbitrary"),
        name="proj",
    )(x2, shift, scale, w_bf, b.reshape(1, n))


def _s5_tables(a_re, a_im, log_dt, b_re, b_im, c_re, c_im, d_skip, chunk):
    hp = lax.Precision.HIGHEST
    g, p = a_re.shape
    hw = b_re.shape[-1]
    w = chunk * hw
    dt = jnp.exp(log_dt.astype(F32))[:, None, None]
    lr, li = a_re.astype(F32)[:, :, None], a_im.astype(F32)[:, :, None]

    def power(k):
        mag = jnp.exp(k * lr * dt)
        return mag * jnp.cos(k * li * dt), mag * jnp.sin(k * li * dt)

    a1_re, a1_im = power(1.0)
    den = lr * lr + li * li
    nr, ni = a1_re - 1.0, a1_im
    z_re = (nr * lr + ni * li) / den
    z_im = (ni * lr - nr * li) / den
    br, bi = b_re.astype(F32), b_im.astype(F32)
    bb_re = z_re * br - z_im * bi
    bb_im = z_re * bi + z_im * br
    lag = (jnp.arange(w + hw) // hw).astype(F32)[None, None, :]
    pw_re, pw_im = power(lag)
    ct_re = jnp.tile(c_re.astype(F32).transpose(0, 2, 1), (1, 1, chunk + 1))
    ct_im = jnp.tile(c_im.astype(F32).transpose(0, 2, 1), (1, 1, chunk + 1))
    ca_re = ct_re * pw_re - ct_im * pw_im
    ca_im = ct_re * pw_im + ct_im * pw_re
    kt = (jnp.einsum('gpj,gpl->gjl', bb_re, ca_re[:, :, :w], precision=hp)
          - jnp.einsum('gpj,gpl->gjl', bb_im, ca_im[:, :, :w], precision=hp))
    m = jnp.stack([jnp.pad(kt[:, :, :w - s * hw], ((0, 0), (0, 0), (s * hw, 0))) for s in range(chunk)],
                  axis=1).reshape(g, w, w)
    rev = (chunk - 1 - jnp.arange(w) // hw).astype(F32)[None, None, :]
    rv_re, rv_im = power(rev)
    bt_re, bt_im = jnp.tile(bb_re, (1, 1, chunk)), jnp.tile(bb_im, (1, 1, chunk))
    pt_re = rv_re * bt_re - rv_im * bt_im
    pt_im = rv_re * bt_im + rv_im * bt_re
    qo_re, qo_im = ca_re[:, :, hw:], -ca_im[:, :, hw:]
    al_re, al_im = power(float(chunk))
    al_re, al_im = al_re.reshape(g, 1, p), al_im.reshape(g, 1, p)
    d_t = jnp.tile(d_skip.astype(F32)[:, None, :], (1, 1, chunk))
    return m, pt_re, pt_im, qo_re, qo_im, al_re, al_im, d_t


def _s5_kernel(u_ref, m_ref, pre_ref, pim_ref, qre_ref, qim_ref, are_ref, aim_ref, d_ref, y_ref,
               zre, zim, sre, sim, *, bsz):
    u = u_ref[0]
    rows = u.shape[0]
    nc = rows // bsz
    y = _dot(u, m_ref[0]) + d_ref[0] * u
    nt = (((1,), (1,)), ((), ()))
    zre[...] = lax.dot_general(u, pre_ref[0], nt, preferred_element_type=F32)
    zim[...] = lax.dot_general(u, pim_ref[0], nt, preferred_element_type=F32)
    ar, ai = are_ref[0], aim_ref[0]
    npair = zre.shape[1]

    def step(c, carry):
        nxt = []
        for b in range(bsz):
            sr, si = carry[2 * b], carry[2 * b + 1]
            row = pl.ds(b * nc + c, 1)
            sre[row, :] = sr
            sim[row, :] = si
            nxt.append(ar * sr - ai * si + zre[row, :])
            nxt.append(ar * si + ai * sr + zim[row, :])
        return tuple(nxt)

    zero = jnp.zeros((1, npair), F32)
    lax.fori_loop(0, nc, step, tuple(zero for _ in range(2 * bsz)))
    y = y + _dot(sre[...], qre_ref[0]) + _dot(sim[...], qim_ref[0])
    y_ref[0] = jax.nn.gelu(y, approximate=True)


def _pack_kernel(u_ref, o_ref, *, chunk):
    g, cb, _ = o_ref.shape
    n_lt = u_ref.shape[0]
    per = g // n_lt
    hw = LANES // per
    for s in range(chunk):
        for q in range(n_lt):
            rows = u_ref[q, pl.ds(s, cb, stride=chunk), :]
            for i in range(per):
                o_ref[q * per + i, :, s * hw:(s + 1) * hw] = rows[:, i * hw:(i + 1) * hw]


def _unpack_kernel(y_ref, o_ref, tmp, *, chunk):
    g, cb, _ = y_ref.shape
    n_lt = o_ref.shape[0]
    per = g // n_lt
    hw = LANES // per
    for s in range(chunk):
        for q in range(n_lt):
            for i in range(per):
                tmp[:, i * hw:(i + 1) * hw] = y_ref[q * per + i, :, s * hw:(s + 1) * hw]
            o_ref[q, pl.ds(s, cb, stride=chunk), :] = tmp[...]


def _s5(u, tables, bsz, chunk):
    m, p_re, p_im, q_re, q_im, a_re, a_im, d_t = tables
    n_lt, t, _ = u.shape
    g, cw, _ = m.shape
    npair = p_re.shape[1]
    rows = t // chunk
    cb = min(64, rows)
    by_chunk = pl.BlockSpec((n_lt, cb * chunk, LANES), lambda i: (0, i, 0))
    by_group = pl.BlockSpec((g, cb, cw), lambda i: (0, i, 0))
    ut = pl.pallas_call(
        functools.partial(_pack_kernel, chunk=chunk), grid=(rows // cb,), in_specs=[by_chunk],
        out_specs=by_group, out_shape=jax.ShapeDtypeStruct((g, rows, cw), F32),
        compiler_params=_cparams("arbitrary"), name="s5_pack",
    )(u)
    gspec = lambda *shape: pl.BlockSpec((1,) + shape, lambda i: (i, 0, 0))
    yt = pl.pallas_call(
        functools.partial(_s5_kernel, bsz=bsz),
        grid=(g,),
        in_specs=[gspec(rows, cw), gspec(cw, cw), gspec(npair, cw), gspec(npair, cw),
                  gspec(npair, cw), gspec(npair, cw), gspec(1, npair), gspec(1, npair), gspec(1, cw)],
        out_specs=gspec(rows, cw),
        out_shape=jax.ShapeDtypeStruct((g, rows, cw), F32),
        scratch_shapes=[pltpu.VMEM((rows, npair), F32) for _ in range(4)],
        compiler_params=_cparams("arbitrary"),
        name="s5",
    )(ut, m, p_re, p_im, q_re, q_im, a_re, a_im, d_t)
    return pl.pallas_call(
        functools.partial(_unpack_kernel, chunk=chunk), grid=(rows // cb,), in_specs=[by_group],
        out_specs=by_chunk, out_shape=jax.ShapeDtypeStruct((n_lt, t, LANES), F32),
        scratch_shapes=[pltpu.VMEM((cb, LANES), F32)],
        compiler_params=_cparams("arbitrary"), name="s5_unpack",
    )(yt)


def _conv_window(win, w_ref, b_ref, g_ref, beta_ref, ts, span):
    sub = 8
    base = CONV_HALO - (span - 1)
    acc = None
    for q in range(sub):
        part = None
        for k in range(span):
            if (base + k) % sub == q:
                lo = base + k - q
                term = win[lo:lo + ts + sub, :] * w_ref[k:k + 1, :]
                part = term if part is None else part + term
        if part is not None:
            part = part[q:q + ts, :]
            acc = part if acc is None else acc + part
    y = _ln(acc + b_ref[...]) * g_ref[...] + beta_ref[...]
    return y * jax.nn.sigmoid(y)


def _branch_kernel(ys_ref, v0_ref, vn_ref, gs_ref, wgate_ref, wup_ref, wco_ref, wdw_ref, bdw_ref, lng_ref,
                   lnb_ref, o_ref, win, cact, *, span, tiles_per_batch):
    i = pl.program_id(0)
    tm = o_ref.shape[0]
    zeros = functools.partial(jnp.zeros, dtype=F32)

    def conv_tile(v_ref, slot):
        win[CONV_HALO:CONV_HALO + tm, :] = v_ref[...]
        act = _conv_window(win, wdw_ref, bdw_ref, lng_ref, lnb_ref, tm, span)
        cact[slot] = act.astype(BF16)
        win[0:CONV_HALO, :] = win[tm:tm + CONV_HALO, :]

    @pl.when(i == 0)
    def _():
        win[0:CONV_HALO, :] = zeros((CONV_HALO, win.shape[1]))
        win[CONV_HALO + tm:, :] = zeros((win.shape[0] - CONV_HALO - tm, win.shape[1]))
        conv_tile(v0_ref, 0)

    @pl.when((i + 1) % tiles_per_batch == 0)
    def _():
        win[0:CONV_HALO, :] = zeros((CONV_HALO, win.shape[1]))

    conv_tile(vn_ref, (i + 1) % 2)
    ys = jnp.concatenate([ys_ref[q] for q in range(ys_ref.shape[0])], axis=-1)
    glu = ys * jax.nn.sigmoid(_dot(ys.astype(BF16), wgate_ref[...]))
    y_s5 = _dot(glu.astype(BF16), wup_ref[...])
    y_conv = _dot(cact[i % 2], wco_ref[...])
    d = y_s5.shape[1]
    merged = gs_ref[:, :d].astype(F32) * y_s5 + gs_ref[:, d:].astype(F32) * y_conv
    o_ref[...] = merged.astype(BF16)


def _branch(ys, v, gates, w_gate, w_up, w_co, w_dw, b_dw, ln_g, ln_b, seq, tm):
    n_lt, t, _ = ys.shape
    cc = v.shape[1]
    d = w_up.shape[1]
    span = w_dw.shape[0]
    n_tiles = t // tm
    full = lambda a: pl.BlockSpec(a.shape, lambda i: (0, 0))
    row = lambda a: a.reshape(1, cc)
    w_dw, b_dw, ln_g, ln_b = w_dw.reshape(span, cc), row(b_dw), row(ln_g), row(ln_b)
    kern = functools.partial(_branch_kernel, span=span, tiles_per_batch=seq // tm)
    return pl.pallas_call(
        kern,
        grid=(n_tiles,),
        in_specs=[pl.BlockSpec((n_lt, tm, LANES), lambda i: (0, i, 0)),
                  pl.BlockSpec((tm, cc), lambda i: (0, 0)),
                  pl.BlockSpec((tm, cc), lambda i: (jnp.minimum(i + 1, n_tiles - 1), 0)),
                  pl.BlockSpec((tm, 2 * d), lambda i: (i, 0)),
                  full(w_gate), full(w_up), full(w_co), full(w_dw), full(b_dw), full(ln_g), full(ln_b)],
        out_specs=pl.BlockSpec((tm, d), lambda i: (i, 0)),
        out_shape=jax.ShapeDtypeStruct((t, d), BF16),
        scratch_shapes=[pltpu.VMEM((CONV_HALO + tm + 8, cc), F32), pltpu.VMEM((2, tm, cc), BF16)],
        compiler_params=_cparams("arbitrary"),
        name="branch",
    )(ys, v, v, gates, w_gate, w_up, w_co, w_dw, b_dw, ln_g, ln_b)


def _mix_kernel(m_ref, x_ref, gate_ref, g1_ref, b1_ref, scale2_ref, shift2_ref, wout_ref, wr_ref, br_ref,
                x1_ref, h2_ref, route_ref, routet_ref, cnt_ref, run, *, alpha, n_exp, n_grp):
    @pl.when(pl.program_id(0) == 0)
    def _():
        run[...] = jnp.zeros_like(run)

    mix = _dot(m_ref[...], wout_ref[...])
    x1 = _ln(alpha * x_ref[...] + (1.0 + gate_ref[0]) * mix) * g1_ref[...] + b1_ref[...]
    x1_ref[...] = x1
    h2 = _ln(x1) * (1.0 + scale2_ref[0]) + shift2_ref[0]
    _store_rows(h2_ref, 0, h2.shape[0], h2.shape[1] // (2 * LANES), _pack_pairs(h2))

    logits = _dot(h2, wr_ref[...]) + br_ref[...]
    tm = logits.shape[0]
    per = n_exp // n_grp
    lane = lax.broadcasted_iota(jnp.int32, logits.shape, 1)
    big = jnp.int32(LANES)
    neg = jnp.float32(-jnp.inf)

    def first_max(val):
        top = jnp.max(val, axis=-1, keepdims=True)
        return top, jnp.min(jnp.where(val == top, lane, big), axis=-1, keepdims=True)

    gmask = (lane >= n_exp) & (lane < n_exp + n_grp)
    gtop, glane = first_max(jnp.where(gmask, logits, neg))
    grp = glane - n_exp
    grp_w = 1.0 / jnp.sum(jnp.where(gmask, jnp.exp(logits - gtop), 0.0), axis=-1, keepdims=True)
    emask = (lane >= grp * per) & (lane < grp * per + per)
    el = jnp.where(emask, logits, neg)
    t1, e1 = first_max(el)
    t2, e2 = first_max(jnp.where(lane == e1, neg, el))
    ex = jnp.exp(t2 - t1)
    w1 = grp_w / (1.0 + ex)
    w2 = grp_w * ex / (1.0 + ex)

    oh1 = (lane == e1).astype(F32)
    oh2 = (lane == e2).astype(F32)
    both = oh1 + oh2
    r_i = lax.broadcasted_iota(jnp.int32, (tm, tm), 0)
    c_i = lax.broadcasted_iota(jnp.int32, (tm, tm), 1)
    before = _dot((r_i > c_i).astype(F32), both) + run[...]
    rank1 = jnp.sum(before * oh1, axis=-1, keepdims=True)
    rank2 = jnp.sum(before * oh2, axis=-1, keepdims=True)
    run[...] = run[...] + jnp.sum(both, axis=0, keepdims=True)
    cnt_ref[...] = run[...]

    route = jnp.where(lane == 0, e1.astype(F32), 0.0)
    route = jnp.where(lane == 1, e2.astype(F32), route)
    route = jnp.where(lane == 2, w1, route)
    route = jnp.where(lane == 3, w2, route)
    route = jnp.where(lane == 4, rank1, route)
    route = jnp.where(lane == 5, rank2, route)
    route_ref[...] = route
    routet_ref[...] = route.T[0:routet_ref.shape[0], :]


def _mix(merged, x2, gate1, g1, b1, scale2, shift2, w_out, w_r, b_r, seq, tm, alpha, n_exp, n_grp):
    t, d = x2.shape
    k = d // (2 * LANES)
    tiles_per_batch = seq // tm
    tile = pl.BlockSpec((tm, d), lambda i: (i, 0))
    per_batch = pl.BlockSpec((1, 1, d), lambda i: (i // tiles_per_batch, 0, 0))
    const = lambda a: pl.BlockSpec(a.shape, lambda i: (0, 0))
    g1, b1 = g1.reshape(1, d), b1.reshape(1, d)
    kern = functools.partial(_mix_kernel, alpha=alpha, n_exp=n_exp, n_grp=n_grp)
    return pl.pallas_call(
        kern,
        grid=(t // tm,),
        in_specs=[tile, tile, per_batch, const(g1), const(b1), per_batch, per_batch,
                  const(w_out), const(w_r), const(b_r)],
        out_specs=[tile, pl.BlockSpec((tm * k, LANES), lambda i: (i, 0)),
                   pl.BlockSpec((tm, LANES), lambda i: (i, 0)),
                   pl.BlockSpec((8, tm), lambda i: (0, i)),
                   pl.BlockSpec((1, LANES), lambda i: (0, 0))],
        out_shape=[jax.ShapeDtypeStruct((t, d), F32), jax.ShapeDtypeStruct((t * k, LANES), jnp.uint32),
                   jax.ShapeDtypeStruct((t, LANES), F32), jax.ShapeDtypeStruct((8, t), F32),
                   jax.ShapeDtypeStruct((1, LANES), F32)],
        scratch_shapes=[pltpu.VMEM((1, LANES), F32)],
        compiler_params=_cparams("arbitrary"),
        name="mix_route",
    )(merged, x2, gate1, g1, b1, scale2, shift2, w_out, w_r, b_r)


def _moe_kernel(bexp_ref, nused_ref, nvalid_ref, tok_ref, tok_next_ref, slot_ref, h_hbm, wg_ref, wu_ref, wd_ref,
                y_hbm, xbuf, ybuf, gsem, ssem):
    del bexp_ref
    rows = tok_ref.shape[2]
    k = xbuf.shape[1] // rows
    group = MOE_DMA_GROUP
    b = pl.program_id(0)
    last = pl.num_programs(0) - 1
    n_used = nused_ref[0]
    cur, nxt = b % 2, (b + 1) % 2
    nv_cur = nvalid_ref[b]
    nv_next = nvalid_ref[jnp.minimum(b + 1, last)]

    def per_group(n_valid, fn):
        for g0 in range(0, rows, group):
            pl.when(g0 < n_valid)(functools.partial(fn, g0))

    def gather_start(idx_ref, buf, n_valid):
        def start(g0):
            for r in range(g0, g0 + group):
                src = h_hbm.at[pl.ds(pl.multiple_of(idx_ref[0, 0, r], k), k)]
                pltpu.make_async_copy(src, xbuf.at[buf, pl.ds(r * k, k)], gsem.at[buf]).start()
        per_group(n_valid, start)

    def gather_wait(buf, n_valid):
        def wait(g0):
            pltpu.make_async_copy(h_hbm.at[pl.ds(0, group * k)], xbuf.at[buf, pl.ds(g0 * k, group * k)],
                                  gsem.at[buf]).wait()
        per_group(n_valid, wait)

    def scatter_start(buf, n_valid):
        def start(g0):
            for r in range(g0, g0 + group):
                dst = y_hbm.at[pl.ds(pl.multiple_of(slot_ref[0, 0, r], k), k)]
                pltpu.make_async_copy(ybuf.at[buf, pl.ds(r * k, k)], dst, ssem.at[buf]).start()
        per_group(n_valid, start)

    def scatter_wait(buf, n_valid):
        def wait(g0):
            pltpu.make_async_copy(ybuf.at[buf, pl.ds(g0 * k, group * k)], y_hbm.at[pl.ds(0, group * k)],
                                  ssem.at[buf]).wait()
        per_group(n_valid, wait)

    @pl.when(b == 0)
    def _():
        xbuf[...] = jnp.zeros(xbuf.shape, xbuf.dtype)
        ybuf[1] = jnp.zeros(ybuf.shape[1:], ybuf.dtype)
        spare = pltpu.make_async_copy(ybuf.at[1], y_hbm.at[pl.ds(y_hbm.shape[0] - rows * k, rows * k)], ssem.at[1])
        spare.start()
        spare.wait()
        gather_start(tok_ref, 0, nv_cur)

    @pl.when((b >= 2) & (b < n_used))
    def _():
        scatter_wait(cur, nvalid_ref[jnp.maximum(b - 2, 0)])

    @pl.when(b < n_used)
    def _():
        gather_start(tok_next_ref, nxt, nv_next)
        gather_wait(cur, nv_cur)
        x = _unpack_pairs(_load_rows(xbuf.at[cur], 0, rows, k, k))
        hg = _dot(x, wg_ref[0])
        act = hg * jax.nn.sigmoid(hg) * _dot(x, wu_ref[0])
        _store_rows(ybuf.at[cur], 0, rows, k, _pack_pairs(_dot(act, wd_ref[0])))
        scatter_start(cur, nv_cur)

    @pl.when(b == n_used - 1)
    def _():
        gather_wait(nxt, nv_next)
        scatter_wait(cur, nv_cur)

        @pl.when(b >= 1)
        def _():
            scatter_wait(nxt, nvalid_ref[jnp.maximum(b - 1, 0)])


def _moe(h2, row_tok, row_slot, block_expert, n_used, n_valid, w_g, w_u, w_d, n_slots):
    de, d = w_d.shape[-2:]
    k = d // (2 * LANES)
    n_blocks = row_tok.shape[0]
    smem = lambda off: pl.BlockSpec((1, 1, MOE_ROWS),
                                    lambda b, e, n, v: (jnp.minimum(b + off, n_blocks - 1), 0, 0),
                                    memory_space=pltpu.SMEM)
    grid_spec = pltpu.PrefetchScalarGridSpec(
        num_scalar_prefetch=3,
        grid=(n_blocks,),
        in_specs=[smem(0), smem(1), smem(0),
                  pl.BlockSpec(memory_space=pl.ANY),
                  pl.BlockSpec((1, d, de), lambda b, e, n, v: (e[b], 0, 0)),
                  pl.BlockSpec((1, d, de), lambda b, e, n, v: (e[b], 0, 0)),
                  pl.BlockSpec((1, de, d), lambda b, e, n, v: (e[b], 0, 0))],
        out_specs=pl.BlockSpec(memory_space=pl.ANY),
        scratch_shapes=[pltpu.VMEM((2, MOE_ROWS * k, LANES), jnp.uint32),
                        pltpu.VMEM((2, MOE_ROWS * k, LANES), jnp.uint32),
                        pltpu.SemaphoreType.DMA((2,)), pltpu.SemaphoreType.DMA((2,))],
    )
    return pl.pallas_call(
        _moe_kernel,
        grid_spec=grid_spec,
        out_shape=jax.ShapeDtypeStruct(((n_slots + MOE_ROWS) * k, LANES), jnp.uint32),
        compiler_params=_cparams("arbitrary"),
        name="moe",
    )(block_expert, n_used, n_valid, row_tok, row_tok, row_slot, h2, w_g, w_u, w_d)


def _dispatch_tables(route_t, counts, n_exp, k):
    t = route_t.shape[1]
    n_slots = t * TOP_K
    n_blocks = n_slots // MOE_ROWS + n_exp
    slot_e = route_t[0:TOP_K].astype(jnp.int32)
    rank = route_t[4:4 + TOP_K].astype(jnp.int32)
    cnt = counts[0, :n_exp].astype(jnp.int32)
    padded = (cnt + MOE_ROWS - 1) // MOE_ROWS * MOE_ROWS
    pends = jnp.cumsum(padded)
    pstarts = pends - padded
    experts = jnp.arange(n_exp, dtype=jnp.int32)[:, None, None]
    dest = (jnp.sum(jnp.where(slot_e[None] == experts, pstarts[:, None, None], 0), axis=0) + rank).reshape(-1)
    slot_id = (TOP_K * jnp.arange(t, dtype=jnp.int32)[None, :] + jnp.arange(TOP_K, dtype=jnp.int32)[:, None])
    spare = n_slots + jnp.arange(n_blocks * MOE_ROWS, dtype=jnp.int32) % MOE_ROWS
    row_slot = spare.at[dest].set(slot_id.reshape(-1))
    row_tok = jnp.where(row_slot < n_slots, row_slot // TOP_K, 0)
    block_start = jnp.arange(n_blocks, dtype=jnp.int32) * MOE_ROWS
    block_expert = jnp.minimum(jnp.sum(block_start[None, :] >= pends[:, None], axis=0), n_exp - 1).astype(jnp.int32)
    valid_end = jnp.sum(jnp.where(block_expert[None, :] == experts[:, :, 0], (pstarts + cnt)[:, None], 0), axis=0)
    n_valid = jnp.clip(valid_end - block_start, 0, MOE_ROWS).astype(jnp.int32)
    n_used = (pends[-1:] // MOE_ROWS).astype(jnp.int32)
    shape3 = (n_blocks, 1, MOE_ROWS)
    return (row_tok * k).reshape(shape3), (row_slot * k).reshape(shape3), block_expert, n_used, n_valid


def _final_kernel(y_ref, route_ref, x1_ref, gate_ref, g_ref, b_ref, o_ref, *, alpha):
    tm, d = x1_ref.shape
    k = d // (2 * LANES)
    route = route_ref[...]
    y0 = _unpack_pairs(_load_rows(y_ref, 0, tm, k, TOP_K * k))
    y1 = _unpack_pairs(_load_rows(y_ref, k, tm, k, TOP_K * k))
    ffn = route[:, 2:3] * y0 + route[:, 3:4] * y1
    o_ref[...] = _ln(alpha * x1_ref[...] + (1.0 + gate_ref[0]) * ffn) * g_ref[...] + b_ref[...]


def _final(y_slots, route, x1, gate2, g2, b2, seq, tm, alpha):
    t, d = x1.shape
    k = d // (2 * LANES)
    tiles_per_batch = seq // tm
    return pl.pallas_call(
        functools.partial(_final_kernel, alpha=alpha),
        grid=(t // tm,),
        in_specs=[pl.BlockSpec((tm * TOP_K * k, LANES), lambda i: (i, 0)),
                  pl.BlockSpec((tm, LANES), lambda i: (i, 0)),
                  pl.BlockSpec((tm, d), lambda i: (i, 0)),
                  pl.BlockSpec((1, 1, d), lambda i: (i // tiles_per_batch, 0, 0)),
                  pl.BlockSpec((1, d), lambda i: (0, 0)),
                  pl.BlockSpec((1, d), lambda i: (0, 0))],
        out_specs=pl.BlockSpec((tm, d), lambda i: (i, 0)),
        out_shape=jax.ShapeDtypeStruct((t, d), F32),
        compiler_params=_cparams("arbitrary"),
        name="final",
    )(y_slots, route, x1, gate2, g2.reshape(1, d), b2.reshape(1, d))


def _tiles(seq):
    return dict(proj=min(1024, seq), branch=min(512, seq), mix=min(512, seq), final=min(512, seq))


def kernel(x, c, w_ada, b_ada, w_in, b_in, s5_a_re, s5_a_im, s5_log_dt, s5_b_re, s5_b_im, s5_c_re, s5_c_im, s5_d, w_s5_gate, w_s5_up, conv_dw, conv_dw_b, conv_ln_g, conv_ln_b, w_conv_out, w_out, ln1_g, ln1_b, w_route_group, b_route_group, w_route_expert, b_route_expert, w_exp_gate, w_exp_up, w_exp_down, ln2_g, ln2_b):
    bsz, seq, d = x.shape
    t = bsz * seq
    depth = w_ada.shape[0]
    alpha = (2.0 * depth) ** 0.25
    s5_width = w_s5_gate.shape[1]
    conv_ch = conv_dw.shape[-1]
    n_grp = w_route_group.shape[-1]
    n_exp = w_route_expert.shape[-1]
    tl = _tiles(seq)
    tn = min(512, s5_width)
    x2 = x.reshape(t, d)
    for l in range(depth):
        mod = _ada(c, w_ada[l], b_ada[l])
        shift1, scale1, gate1, shift2, scale2, gate2 = [
            m.reshape(bsz, 1, d) for m in jnp.split(mod, 6, axis=-1)]

        u, v, gates = _proj(x2, shift1, scale1, w_in[l].astype(BF16), b_in[l], s5_width, conv_ch, seq,
                            tl["proj"], tn)
        tables = _s5_tables(s5_a_re[l], s5_a_im[l], s5_log_dt[l], s5_b_re[l], s5_b_im[l],
                            s5_c_re[l], s5_c_im[l], s5_d[l], S5_CHUNK)
        ys = _s5(u, tables, bsz, S5_CHUNK)
        merged = _branch(ys, v, gates, w_s5_gate[l].astype(BF16), w_s5_up[l].astype(BF16),
                         w_conv_out[l].astype(BF16), conv_dw[l], conv_dw_b[l], conv_ln_g[l], conv_ln_b[l],
                         seq, tl["branch"])

        w_r = jnp.zeros((d, LANES), F32).at[:, :n_exp].set(w_route_expert[l])
        w_r = w_r.at[:, n_exp:n_exp + n_grp].set(w_route_group[l])
        b_r = jnp.zeros((1, LANES), F32).at[0, :n_exp].set(b_route_expert[l])
        b_r = b_r.at[0, n_exp:n_exp + n_grp].set(b_route_group[l])
        x1, h2, route, route_t, counts = _mix(merged, x2, gate1, ln1_g[l], ln1_b[l], scale2, shift2,
                                              w_out[l].astype(BF16), w_r, b_r, seq, tl["mix"], alpha, n_exp, n_grp)
        row_tok, row_slot, block_expert, n_used, n_valid = _dispatch_tables(route_t, counts, n_exp,
                                                                            d // (2 * LANES))
        y_slots = _moe(h2, row_tok, row_slot, block_expert, n_used, n_valid, w_exp_gate[l], w_exp_up[l],
                       w_exp_down[l], t * TOP_K)
        x2 = _final(y_slots, route, x1, gate2, ln2_g[l], ln2_b[l], seq, tl["final"], alpha)
    return x2.reshape(bsz, seq, d)
```

```python
import functools

import jax
import jax.numpy as jnp
from jax import lax
from jax.experimental import pallas as pl
from jax.experimental.pallas import tpu as pltpu

F32 = jnp.float32
BF16 = jnp.bfloat16
LN_EPS = 1e-5
TOP_K = 2
LANES = 128
S5_CHUNK = 16
CONV_HALO = 32
MOE_ROWS = 256
MOE_DMA_GROUP = 64
VMEM_LIMIT = 56 * 1024 * 1024


def _cparams(*sem):
    return pltpu.CompilerParams(dimension_semantics=sem, vmem_limit_bytes=VMEM_LIMIT)


def _ln(x):
    mu = jnp.mean(x, axis=-1, keepdims=True)
    xc = x - mu
    var = jnp.mean(xc * xc, axis=-1, keepdims=True)
    return xc * lax.rsqrt(var + LN_EPS)


def _dot(a, b):
    return jnp.dot(a, b, preferred_element_type=F32)


def _sigmoid(x):
    return 0.5 * jnp.tanh(0.5 * x) + 0.5


def _pack_pairs(a):
    w = a.shape[1] // 2
    hi = lax.bitcast_convert_type(a[:, :w].astype(BF16).astype(F32), jnp.uint32)
    lo = lax.bitcast_convert_type(a[:, w:].astype(BF16).astype(F32), jnp.uint32)
    return hi | (lo >> 16)


def _unpack_pairs(words):
    hi = lax.bitcast_convert_type(words & jnp.uint32(0xFFFF0000), F32)
    lo = lax.bitcast_convert_type(words << 16, F32)
    return jnp.concatenate([hi, lo], axis=-1)


def _store_rows(ref, first, n, k, packed):
    for j in range(k):
        ref[pl.ds(first + j, n, stride=k), :] = packed[:, j * LANES:(j + 1) * LANES]


def _load_rows(ref, first, n, k, stride):
    return jnp.concatenate([ref[pl.ds(first + j, n, stride=stride), :] for j in range(k)], axis=-1)


def _ada_kernel(c_ref, w_ref, b_ref, o_ref):
    c = c_ref[...]
    o_ref[...] = _dot(c * jax.nn.sigmoid(c), w_ref[...]) + b_ref[...]


def _ada(c, w, b, tn=1024):
    bsz, d = c.shape
    n = w.shape[1]
    rows = 8
    cp = jnp.zeros((rows, d), F32).at[:bsz].set(c)
    out = pl.pallas_call(
        _ada_kernel,
        grid=(n // tn,),
        in_specs=[pl.BlockSpec((rows, d), lambda j: (0, 0)),
                  pl.BlockSpec((d, tn), lambda j: (0, j)),
                  pl.BlockSpec((1, tn), lambda j: (0, j))],
        out_specs=pl.BlockSpec((rows, tn), lambda j: (0, j)),
        out_shape=jax.ShapeDtypeStruct((rows, n), F32),
        compiler_params=_cparams("arbitrary"),
        name="ada",
    )(cp, w, b.reshape(1, n))
    return out[:bsz]


def _proj_kernel(x_ref, shift_ref, scale_ref, w_ref, b_ref, u_ref, v_ref, g_ref, h_scr, a_scr,
                 *, n_u, n_a):
    j = pl.program_id(1)

    @pl.when(j == 0)
    def _():
        h = _ln(x_ref[...]) * (1.0 + scale_ref[0]) + shift_ref[0]
        h_scr[...] = h.astype(BF16)

    p = _dot(h_scr[...], w_ref[...]) + b_ref[...]

    @pl.when(j < n_u)
    def _():
        for q in range(u_ref.shape[0]):
            u_ref[q] = p[:, q * LANES:(q + 1) * LANES]

    @pl.when((j >= n_u) & (j < n_u + n_a))
    def _():
        a_scr[j - n_u] = p

    @pl.when((j >= n_u + n_a) & (j < n_u + 2 * n_a))
    def _():
        v_ref[...] = a_scr[j - n_u - n_a] * _sigmoid(p)

    @pl.when(j >= n_u + 2 * n_a)
    def _():
        g_ref[...] = _sigmoid(p.astype(BF16))


def _proj(x2, shift, scale, w_bf, b, s5_width, conv_ch, seq, tm, tn):
    t, d = x2.shape
    n = w_bf.shape[1]
    n_u, n_a = s5_width // tn, conv_ch // tn
    assert n_u == 1, "the S5 input must be one column tile"
    n_lt = s5_width // LANES
    n_g = (n - s5_width - 2 * conv_ch) // tn
    tiles_per_batch = seq // tm
    g0 = n_u + 2 * n_a
    kern = functools.partial(_proj_kernel, n_u=n_u, n_a=n_a)
    return pl.pallas_call(
        kern,
        grid=(t // tm, n // tn),
        in_specs=[pl.BlockSpec((tm, d), lambda i, j: (i, 0)),
                  pl.BlockSpec((1, 1, d), lambda i, j: (i // tiles_per_batch, 0, 0)),
                  pl.BlockSpec((1, 1, d), lambda i, j: (i // tiles_per_batch, 0, 0)),
                  pl.BlockSpec((d, tn), lambda i, j: (0, j)),
                  pl.BlockSpec((1, tn), lambda i, j: (0, j))],
        out_specs=[pl.BlockSpec((n_lt, tm, LANES), lambda i, j: (0, i, 0)),
                   pl.BlockSpec((tm, tn), lambda i, j: (i, jnp.clip(j - n_u - n_a, 0, n_a - 1))),
                   pl.BlockSpec((tm, tn), lambda i, j: (i, jnp.clip(j - g0, 0, n_g - 1)))],
        out_shape=[jax.ShapeDtypeStruct((n_lt, t, LANES), F32),
                   jax.ShapeDtypeStruct((t, conv_ch), F32),
                   jax.ShapeDtypeStruct((t, n_g * tn), BF16)],
        scratch_shapes=[pltpu.VMEM((tm, d), BF16), pltpu.VMEM((n_a, tm, tn), F32)],
        compiler_params=_cparams("arbitrary", "arbitrary"),
        name="proj",
    )(x2, shift, scale, w_bf, b.reshape(1, n))


def _s5_tables(a_re, a_im, log_dt, b_re, b_im, c_re, c_im, d_skip, chunk):
    hp = lax.Precision.HIGHEST
    g, p = a_re.shape
    hw = b_re.shape[-1]
    w = chunk * hw
    dt = jnp.exp(log_dt.astype(F32))[:, None, None]
    lr, li = a_re.astype(F32)[:, :, None], a_im.astype(F32)[:, :, None]

    def power(k):
        mag = jnp.exp(k * lr * dt)
        return mag * jnp.cos(k * li * dt), mag * jnp.sin(k * li * dt)

    a1_re, a1_im = power(1.0)
    den = lr * lr + li * li
    nr, ni = a1_re - 1.0, a1_im
    z_re = (nr * lr + ni * li) / den
    z_im = (ni * lr - nr * li) / den
    br, bi = b_re.astype(F32), b_im.astype(F32)
    bb_re = z_re * br - z_im * bi
    bb_im = z_re * bi + z_im * br
    lag = (jnp.arange(w + hw) // hw).astype(F32)[None, None, :]
    pw_re, pw_im = power(lag)
    ct_re = jnp.tile(c_re.astype(F32).transpose(0, 2, 1), (1, 1, chunk + 1))
    ct_im = jnp.tile(c_im.astype(F32).transpose(0, 2, 1), (1, 1, chunk + 1))
    ca_re = ct_re * pw_re - ct_im * pw_im
    ca_im = ct_re * pw_im + ct_im * pw_re
    kt = (jnp.einsum('gpj,gpl->gjl', bb_re, ca_re[:, :, :w], precision=hp)
          - jnp.einsum('gpj,gpl->gjl', bb_im, ca_im[:, :, :w], precision=hp))
    m = jnp.stack([jnp.pad(kt[:, :, :w - s * hw], ((0, 0), (0, 0), (s * hw, 0))) for s in range(chunk)],
                  axis=1).reshape(g, w, w)
    rev = (chunk - 1 - jnp.arange(w) // hw).astype(F32)[None, None, :]
    rv_re, rv_im = power(rev)
    bt_re, bt_im = jnp.tile(bb_re, (1, 1, chunk)), jnp.tile(bb_im, (1, 1, chunk))
    pt_re = rv_re * bt_re - rv_im * bt_im
    pt_im = rv_re * bt_im + rv_im * bt_re
    qo_re, qo_im = ca_re[:, :, hw:], -ca_im[:, :, hw:]
    al_re, al_im = power(float(chunk))
    al_re, al_im = al_re.reshape(g, 1, p), al_im.reshape(g, 1, p)
    d_t = jnp.tile(d_skip.astype(F32)[:, None, :], (1, 1, chunk))
    return m, pt_re, pt_im, qo_re, qo_im, al_re, al_im, d_t


def _s5_kernel(u_ref, m_ref, pre_ref, pim_ref, qre_ref, qim_ref, are_ref, aim_ref, d_ref, y_ref,
               zre, zim, sre, sim, *, bsz):
    u = u_ref[0]
    rows = u.shape[0]
    nc = rows // bsz
    y = _dot(u, m_ref[0]) + d_ref[0] * u
    nt = (((1,), (1,)), ((), ()))
    zre[...] = lax.dot_general(u, pre_ref[0], nt, preferred_element_type=F32)
    zim[...] = lax.dot_general(u, pim_ref[0], nt, preferred_element_type=F32)
    ar, ai = are_ref[0], aim_ref[0]
    npair = zre.shape[1]

    def step(c, carry):
        nxt = []
        for b in range(bsz):
            sr, si = carry[2 * b], carry[2 * b + 1]
            row = pl.ds(b * nc + c, 1)
            sre[row, :] = sr
            sim[row, :] = si
            nxt.append(ar * sr - ai * si + zre[row, :])
            nxt.append(ar * si + ai * sr + zim[row, :])
        return tuple(nxt)

    zero = jnp.zeros((1, npair), F32)
    lax.fori_loop(0, nc, step, tuple(zero for _ in range(2 * bsz)))
    y = y + _dot(sre[...], qre_ref[0]) + _dot(sim[...], qim_ref[0])
    y_ref[0] = jax.nn.gelu(y, approximate=True)


def _pack_kernel(u_ref, o_ref, *, chunk):
    g, cb, _ = o_ref.shape
    n_lt = u_ref.shape[0]
    per = g // n_lt
    hw = LANES // per
    for s in range(chunk):
        for q in range(n_lt):
            rows = u_ref[q, pl.ds(s, cb, stride=chunk), :]
            for i in range(per):
                o_ref[q * per + i, :, s * hw:(s + 1) * hw] = rows[:, i * hw:(i + 1) * hw]


def _unpack_kernel(y_ref, o_ref, tmp, *, chunk):
    g, cb, _ = y_ref.shape
    n_lt = o_ref.shape[0]
    per = g // n_lt
    hw = LANES // per
    for s in range(chunk):
        for q in range(n_lt):
            for i in range(per):
                tmp[:, i * hw:(i + 1) * hw] = y_ref[q * per + i, :, s * hw:(s + 1) * hw]
            o_ref[q, pl.ds(s, cb, stride=chunk), :] = tmp[...]


def _s5(u, tables, bsz, chunk):
    m, p_re, p_im, q_re, q_im, a_re, a_im, d_t = tables
    n_lt, t, _ = u.shape
    g, cw, _ = m.shape
    npair = p_re.shape[1]
    rows = t // chunk
    cb = min(64, rows)
    by_chunk = pl.BlockSpec((n_lt, cb * chunk, LANES), lambda i: (0, i, 0))
    by_group = pl.BlockSpec((g, cb, cw), lambda i: (0, i, 0))
    ut = pl.pallas_call(
        functools.partial(_pack_kernel, chunk=chunk), grid=(rows // cb,), in_specs=[by_chunk],
        out_specs=by_group, out_shape=jax.ShapeDtypeStruct((g, rows, cw), F32),
        compiler_params=_cparams("arbitrary"), name="s5_pack",
    )(u)
    gspec = lambda *shape: pl.BlockSpec((1,) + shape, lambda i: (i, 0, 0))
    yt = pl.pallas_call(
        functools.partial(_s5_kernel, bsz=bsz),
        grid=(g,),
        in_specs=[gspec(rows, cw), gspec(cw, cw), gspec(npair, cw), gspec(npair, cw),
                  gspec(npair, cw), gspec(npair, cw), gspec(1, npair), gspec(1, npair), gspec(1, cw)],
        out_specs=gspec(rows, cw),
        out_shape=jax.ShapeDtypeStruct((g, rows, cw), F32),
        scratch_shapes=[pltpu.VMEM((rows, npair), F32) for _ in range(4)],
        compiler_params=_cparams("arbitrary"),
        name="s5",
    )(ut, m, p_re, p_im, q_re, q_im, a_re, a_im, d_t)
    return pl.pallas_call(
        functools.partial(_unpack_kernel, chunk=chunk), grid=(rows // cb,), in_specs=[by_group],
        out_specs=by_chunk, out_shape=jax.ShapeDtypeStruct((n_lt, t, LANES), F32),
        scratch_shapes=[pltpu.VMEM((cb, LANES), F32)],
        compiler_params=_cparams("arbitrary"), name="s5_unpack",
    )(yt)


def _conv_window(win, w_ref, b_ref, g_ref, beta_ref, ts, span):
    sub = 8
    base = CONV_HALO - (span - 1)
    acc = None
    for q in range(sub):
        part = None
        for k in range(span):
            if (base + k) % sub == q:
                lo = base + k - q
                term = win[lo:lo + ts + sub, :] * w_ref[k:k + 1, :]
                part = term if part is None else part + term
        if part is not None:
            part = part[q:q + ts, :]
            acc = part if acc is None else acc + part
    y = _ln(acc + b_ref[...]) * g_ref[...] + beta_ref[...]
    return y * _sigmoid(y)


def _branch_kernel(ys_ref, v0_ref, vn_ref, gs_ref, wgate_ref, wup_ref, wco_ref, wdw_ref, bdw_ref, lng_ref,
                   lnb_ref, o_ref, win, cact, *, span, tiles_per_batch):
    i = pl.program_id(0)
    tm = o_ref.shape[0]
    zeros = functools.partial(jnp.zeros, dtype=F32)

    def conv_tile(v_ref, slot):
        win[CONV_HALO:CONV_HALO + tm, :] = v_ref[...]
        act = _conv_window(win, wdw_ref, bdw_ref, lng_ref, lnb_ref, tm, span)
        cact[slot] = act.astype(BF16)
        win[0:CONV_HALO, :] = win[tm:tm + CONV_HALO, :]

    @pl.when(i == 0)
    def _():
        win[0:CONV_HALO, :] = zeros((CONV_HALO, win.shape[1]))
        win[CONV_HALO + tm:, :] = zeros((win.shape[0] - CONV_HALO - tm, win.shape[1]))
        conv_tile(v0_ref, 0)

    @pl.when((i + 1) % tiles_per_batch == 0)
    def _():
        win[0:CONV_HALO, :] = zeros((CONV_HALO, win.shape[1]))

    conv_tile(vn_ref, (i + 1) % 2)
    ys = jnp.concatenate([ys_ref[q] for q in range(ys_ref.shape[0])], axis=-1)
    glu = ys * _sigmoid(_dot(ys.astype(BF16), wgate_ref[...]))
    y_s5 = _dot(glu.astype(BF16), wup_ref[...])
    y_conv = _dot(cact[i % 2], wco_ref[...])
    d = y_s5.shape[1]
    merged = gs_ref[:, :d].astype(F32) * y_s5 + gs_ref[:, d:].astype(F32) * y_conv
    o_ref[...] = merged.astype(BF16)


def _branch(ys, v, gates, w_gate, w_up, w_co, w_dw, b_dw, ln_g, ln_b, seq, tm):
    n_lt, t, _ = ys.shape
    cc = v.shape[1]
    d = w_up.shape[1]
    span = w_dw.shape[0]
    n_tiles = t // tm
    full = lambda a: pl.BlockSpec(a.shape, lambda i: (0, 0))
    row = lambda a: a.reshape(1, cc)
    w_dw, b_dw, ln_g, ln_b = w_dw.reshape(span, cc), row(b_dw), row(ln_g), row(ln_b)
    kern = functools.partial(_branch_kernel, span=span, tiles_per_batch=seq // tm)
    return pl.pallas_call(
        kern,
        grid=(n_tiles,),
        in_specs=[pl.BlockSpec((n_lt, tm, LANES), lambda i: (0, i, 0)),
                  pl.BlockSpec((tm, cc), lambda i: (0, 0)),
                  pl.BlockSpec((tm, cc), lambda i: (jnp.minimum(i + 1, n_tiles - 1), 0)),
                  pl.BlockSpec((tm, 2 * d), lambda i: (i, 0)),
                  full(w_gate), full(w_up), full(w_co), full(w_dw), full(b_dw), full(ln_g), full(ln_b)],
        out_specs=pl.BlockSpec((tm, d), lambda i: (i, 0)),
        out_shape=jax.ShapeDtypeStruct((t, d), BF16),
        scratch_shapes=[pltpu.VMEM((CONV_HALO + tm + 8, cc), F32), pltpu.VMEM((2, tm, cc), BF16)],
        compiler_params=_cparams("arbitrary"),
        name="branch",
    )(ys, v, v, gates, w_gate, w_up, w_co, w_dw, b_dw, ln_g, ln_b)


def _mix_kernel(m_ref, x_ref, gate_ref, g1_ref, b1_ref, scale2_ref, shift2_ref, wout_ref, wr_ref, br_ref,
                x1_ref, h2_ref, route_ref, routet_ref, cnt_ref, run, *, alpha, n_exp, n_grp):
    @pl.when(pl.program_id(0) == 0)
    def _():
        run[...] = jnp.zeros_like(run)

    mix = _dot(m_ref[...], wout_ref[...])
    x1 = _ln(alpha * x_ref[...] + (1.0 + gate_ref[0]) * mix) * g1_ref[...] + b1_ref[...]
    x1_ref[...] = x1
    h2 = _ln(x1) * (1.0 + scale2_ref[0]) + shift2_ref[0]
    _store_rows(h2_ref, 0, h2.shape[0], h2.shape[1] // (2 * LANES), _pack_pairs(h2))

    logits = _dot(h2, wr_ref[...]) + br_ref[...]
    tm = logits.shape[0]
    per = n_exp // n_grp
    lane = lax.broadcasted_iota(jnp.int32, logits.shape, 1)
    big = jnp.int32(LANES)
    neg = jnp.float32(-jnp.inf)

    def first_max(val):
        top = jnp.max(val, axis=-1, keepdims=True)
        return top, jnp.min(jnp.where(val == top, lane, big), axis=-1, keepdims=True)

    gmask = (lane >= n_exp) & (lane < n_exp + n_grp)
    gtop, glane = first_max(jnp.where(gmask, logits, neg))
    grp = glane - n_exp
    grp_w = 1.0 / jnp.sum(jnp.where(gmask, jnp.exp(logits - gtop), 0.0), axis=-1, keepdims=True)
    emask = (lane >= grp * per) & (lane < grp * per + per)
    el = jnp.where(emask, logits, neg)
    t1, e1 = first_max(el)
    t2, e2 = first_max(jnp.where(lane == e1, neg, el))
    ex = jnp.exp(t2 - t1)
    w1 = grp_w / (1.0 + ex)
    w2 = grp_w * ex / (1.0 + ex)

    oh1 = (lane == e1).astype(F32)
    oh2 = (lane == e2).astype(F32)
    both = oh1 + oh2
    r_i = lax.broadcasted_iota(jnp.int32, (tm, tm), 0)
    c_i = lax.broadcasted_iota(jnp.int32, (tm, tm), 1)
    before = _dot((r_i > c_i).astype(F32), both) + run[...]
    rank1 = jnp.sum(before * oh1, axis=-1, keepdims=True)
    rank2 = jnp.sum(before * oh2, axis=-1, keepdims=True)
    run[...] = run[...] + jnp.sum(both, axis=0, keepdims=True)
    cnt_ref[...] = run[...]

    route = jnp.where(lane == 0, e1.astype(F32), 0.0)
    route = jnp.where(lane == 1, e2.astype(F32), route)
    route = jnp.where(lane == 2, w1, route)
    route = jnp.where(lane == 3, w2, route)
    route = jnp.where(lane == 4, rank1, route)
    route = jnp.where(lane == 5, rank2, route)
    route_ref[...] = route
    routet_ref[...] = route.T[0:routet_ref.shape[0], :]


def _mix(merged, x2, gate1, g1, b1, scale2, shift2, w_out, w_r, b_r, seq, tm, alpha, n_exp, n_grp):
    t, d = x2.shape
    k = d // (2 * LANES)
    tiles_per_batch = seq // tm
    tile = pl.BlockSpec((tm, d), lambda i: (i, 0))
    per_batch = pl.BlockSpec((1, 1, d), lambda i: (i // tiles_per_batch, 0, 0))
    const = lambda a: pl.BlockSpec(a.shape, lambda i: (0, 0))
    g1, b1 = g1.reshape(1, d), b1.reshape(1, d)
    kern = functools.partial(_mix_kernel, alpha=alpha, n_exp=n_exp, n_grp=n_grp)
    return pl.pallas_call(
        kern,
        grid=(t // tm,),
        in_specs=[tile, tile, per_batch, const(g1), const(b1), per_batch, per_batch,
                  const(w_out), const(w_r), const(b_r)],
        out_specs=[tile, pl.BlockSpec((tm * k, LANES), lambda i: (i, 0)),
                   pl.BlockSpec((tm, LANES), lambda i: (i, 0)),
                   pl.BlockSpec((8, tm), lambda i: (0, i)),
                   pl.BlockSpec((1, LANES), lambda i: (0, 0))],
        out_shape=[jax.ShapeDtypeStruct((t, d), F32), jax.ShapeDtypeStruct((t * k, LANES), jnp.uint32),
                   jax.ShapeDtypeStruct((t, LANES), F32), jax.ShapeDtypeStruct((8, t), F32),
                   jax.ShapeDtypeStruct((1, LANES), F32)],
        scratch_shapes=[pltpu.VMEM((1, LANES), F32)],
        compiler_params=_cparams("arbitrary"),
        name="mix_route",
    )(merged, x2, gate1, g1, b1, scale2, shift2, w_out, w_r, b_r)


def _moe_kernel(bexp_ref, nused_ref, nvalid_ref, tok_ref, tok_next_ref, slot_ref, h_hbm, wg_ref, wu_ref, wd_ref,
                y_hbm, xbuf, ybuf, gsem, ssem):
    del bexp_ref
    rows = tok_ref.shape[2]
    k = xbuf.shape[1] // rows
    group = MOE_DMA_GROUP
    b = pl.program_id(0)
    last = pl.num_programs(0) - 1
    n_used = nused_ref[0]
    cur, nxt = b % 2, (b + 1) % 2
    nv_cur = nvalid_ref[b]
    nv_next = nvalid_ref[jnp.minimum(b + 1, last)]

    def per_group(n_valid, fn):
        for g0 in range(0, rows, group):
            pl.when(g0 < n_valid)(functools.partial(fn, g0))

    def gather_start(idx_ref, buf, n_valid):
        def start(g0):
            for r in range(g0, g0 + group):
                src = h_hbm.at[pl.ds(pl.multiple_of(idx_ref[0, 0, r], k), k)]
                pltpu.make_async_copy(src, xbuf.at[buf, pl.ds(r * k, k)], gsem.at[buf]).start()
        per_group(n_valid, start)

    def gather_wait(buf, n_valid):
        def wait(g0):
            pltpu.make_async_copy(h_hbm.at[pl.ds(0, group * k)], xbuf.at[buf, pl.ds(g0 * k, group * k)],
                                  gsem.at[buf]).wait()
        per_group(n_valid, wait)

    def scatter_start(buf, n_valid):
        def start(g0):
            for r in range(g0, g0 + group):
                dst = y_hbm.at[pl.ds(pl.multiple_of(slot_ref[0, 0, r], k), k)]
                pltpu.make_async_copy(ybuf.at[buf, pl.ds(r * k, k)], dst, ssem.at[buf]).start()
        per_group(n_valid, start)

    def scatter_wait(buf, n_valid):
        def wait(g0):
            pltpu.make_async_copy(ybuf.at[buf, pl.ds(g0 * k, group * k)], y_hbm.at[pl.ds(0, group * k)],
                                  ssem.at[buf]).wait()
        per_group(n_valid, wait)

    @pl.when(b == 0)
    def _():
        xbuf[...] = jnp.zeros(xbuf.shape, xbuf.dtype)
        ybuf[1] = jnp.zeros(ybuf.shape[1:], ybuf.dtype)
        spare = pltpu.make_async_copy(ybuf.at[1], y_hbm.at[pl.ds(y_hbm.shape[0] - rows * k, rows * k)], ssem.at[1])
        spare.start()
        spare.wait()
        gather_start(tok_ref, 0, nv_cur)

    @pl.when((b >= 2) & (b < n_used))
    def _():
        scatter_wait(cur, nvalid_ref[jnp.maximum(b - 2, 0)])

    @pl.when(b < n_used)
    def _():
        gather_start(tok_next_ref, nxt, nv_next)
        gather_wait(cur, nv_cur)
        x = _unpack_pairs(_load_rows(xbuf.at[cur], 0, rows, k, k))
        hg = _dot(x, wg_ref[0])
        act = hg * _sigmoid(hg) * _dot(x, wu_ref[0])
        _store_rows(ybuf.at[cur], 0, rows, k, _pack_pairs(_dot(act, wd_ref[0])))
        scatter_start(cur, nv_cur)

    @pl.when(b == n_used - 1)
    def _():
        gather_wait(nxt, nv_next)
        scatter_wait(cur, nv_cur)

        @pl.when(b >= 1)
        def _():
            scatter_wait(nxt, nvalid_ref[jnp.maximum(b - 1, 0)])


def _moe(h2, row_tok, row_slot, block_expert, n_used, n_valid, w_g, w_u, w_d, n_slots):
    de, d = w_d.shape[-2:]
    k = d // (2 * LANES)
    n_blocks = row_tok.shape[0]
    smem = lambda off: pl.BlockSpec((1, 1, MOE_ROWS),
                                    lambda b, e, n, v: (jnp.minimum(b + off, n_blocks - 1), 0, 0),
                                    memory_space=pltpu.SMEM)
    grid_spec = pltpu.PrefetchScalarGridSpec(
        num_scalar_prefetch=3,
        grid=(n_blocks,),
        in_specs=[smem(0), smem(1), smem(0),
                  pl.BlockSpec(memory_space=pl.ANY),
                  pl.BlockSpec((1, d, de), lambda b, e, n, v: (e[b], 0, 0)),
                  pl.BlockSpec((1, d, de), lambda b, e, n, v: (e[b], 0, 0)),
                  pl.BlockSpec((1, de, d), lambda b, e, n, v: (e[b], 0, 0))],
        out_specs=pl.BlockSpec(memory_space=pl.ANY),
        scratch_shapes=[pltpu.VMEM((2, MOE_ROWS * k, LANES), jnp.uint32),
                        pltpu.VMEM((2, MOE_ROWS * k, LANES), jnp.uint32),
                        pltpu.SemaphoreType.DMA((2,)), pltpu.SemaphoreType.DMA((2,))],
    )
    return pl.pallas_call(
        _moe_kernel,
        grid_spec=grid_spec,
        out_shape=jax.ShapeDtypeStruct(((n_slots + MOE_ROWS) * k, LANES), jnp.uint32),
        compiler_params=_cparams("arbitrary"),
        name="moe",
    )(block_expert, n_used, n_valid, row_tok, row_tok, row_slot, h2, w_g, w_u, w_d)


def _dispatch_tables(route_t, counts, n_exp, k):
    t = route_t.shape[1]
    n_slots = t * TOP_K
    n_blocks = n_slots // MOE_ROWS + n_exp
    slot_e = route_t[0:TOP_K].astype(jnp.int32)
    rank = route_t[4:4 + TOP_K].astype(jnp.int32)
    cnt = counts[0, :n_exp].astype(jnp.int32)
    padded = (cnt + MOE_ROWS - 1) // MOE_ROWS * MOE_ROWS
    pends = jnp.cumsum(padded)
    pstarts = pends - padded
    experts = jnp.arange(n_exp, dtype=jnp.int32)[:, None, None]
    dest = (jnp.sum(jnp.where(slot_e[None] == experts, pstarts[:, None, None], 0), axis=0) + rank).reshape(-1)
    slot_id = (TOP_K * jnp.arange(t, dtype=jnp.int32)[None, :] + jnp.arange(TOP_K, dtype=jnp.int32)[:, None])
    spare = n_slots + jnp.arange(n_blocks * MOE_ROWS, dtype=jnp.int32) % MOE_ROWS
    row_slot = spare.at[dest].set(slot_id.reshape(-1), unique_indices=True, mode='promise_in_bounds')
    row_tok = jnp.where(row_slot < n_slots, row_slot // TOP_K, 0)
    block_start = jnp.arange(n_blocks, dtype=jnp.int32) * MOE_ROWS
    block_expert = jnp.minimum(jnp.sum(block_start[None, :] >= pends[:, None], axis=0), n_exp - 1).astype(jnp.int32)
    valid_end = jnp.sum(jnp.where(block_expert[None, :] == experts[:, :, 0], (pstarts + cnt)[:, None], 0), axis=0)
    n_valid = jnp.clip(valid_end - block_start, 0, MOE_ROWS).astype(jnp.int32)
    n_used = (pends[-1:] // MOE_ROWS).astype(jnp.int32)
    shape3 = (n_blocks, 1, MOE_ROWS)
    return (row_tok * k).reshape(shape3), (row_slot * k).reshape(shape3), block_expert, n_used, n_valid


def _final_kernel(y_ref, route_ref, x1_ref, gate_ref, g_ref, b_ref, o_ref, *, alpha):
    tm, d = x1_ref.shape
    k = d // (2 * LANES)
    route = route_ref[...]
    y0 = _unpack_pairs(_load_rows(y_ref, 0, tm, k, TOP_K * k))
    y1 = _unpack_pairs(_load_rows(y_ref, k, tm, k, TOP_K * k))
    ffn = route[:, 2:3] * y0 + route[:, 3:4] * y1
    o_ref[...] = _ln(alpha * x1_ref[...] + (1.0 + gate_ref[0]) * ffn) * g_ref[...] + b_ref[...]


def _final(y_slots, route, x1, gate2, g2, b2, seq, tm, alpha):
    t, d = x1.shape
    k = d // (2 * LANES)
    tiles_per_batch = seq // tm
    return pl.pallas_call(
        functools.partial(_final_kernel, alpha=alpha),
        grid=(t // tm,),
        in_specs=[pl.BlockSpec((tm * TOP_K * k, LANES), lambda i: (i, 0)),
                  pl.BlockSpec((tm, LANES), lambda i: (i, 0)),
                  pl.BlockSpec((tm, d), lambda i: (i, 0)),
                  pl.BlockSpec((1, 1, d), lambda i: (i // tiles_per_batch, 0, 0)),
                  pl.BlockSpec((1, d), lambda i: (0, 0)),
                  pl.BlockSpec((1, d), lambda i: (0, 0))],
        out_specs=pl.BlockSpec((tm, d), lambda i: (i, 0)),
        out_shape=jax.ShapeDtypeStruct((t, d), F32),
        compiler_params=_cparams("arbitrary"),
        name="final",
    )(y_slots, route, x1, gate2, g2.reshape(1, d), b2.reshape(1, d))


def _tiles(seq):
    return dict(proj=min(1024, seq), branch=min(512, seq), mix=min(512, seq), final=min(512, seq))


def kernel(x, c, w_ada, b_ada, w_in, b_in, s5_a_re, s5_a_im, s5_log_dt, s5_b_re, s5_b_im, s5_c_re, s5_c_im, s5_d, w_s5_gate, w_s5_up, conv_dw, conv_dw_b, conv_ln_g, conv_ln_b, w_conv_out, w_out, ln1_g, ln1_b, w_route_group, b_route_group, w_route_expert, b_route_expert, w_exp_gate, w_exp_up, w_exp_down, ln2_g, ln2_b):
    bsz, seq, d = x.shape
    t = bsz * seq
    depth = w_ada.shape[0]
    alpha = (2.0 * depth) ** 0.25
    s5_width = w_s5_gate.shape[1]
    conv_ch = conv_dw.shape[-1]
    n_grp = w_route_group.shape[-1]
    n_exp = w_route_expert.shape[-1]
    tl = _tiles(seq)
    tn = min(512, s5_width)
    x2 = x.reshape(t, d)
    for l in range(depth):
        mod = _ada(c, w_ada[l], b_ada[l])
        shift1, scale1, gate1, shift2, scale2, gate2 = [
            m.reshape(bsz, 1, d) for m in jnp.split(mod, 6, axis=-1)]

        u, v, gates = _proj(x2, shift1, scale1, w_in[l].astype(BF16), b_in[l], s5_width, conv_ch, seq,
                            tl["proj"], tn)
        tables = _s5_tables(s5_a_re[l], s5_a_im[l], s5_log_dt[l], s5_b_re[l], s5_b_im[l],
                            s5_c_re[l], s5_c_im[l], s5_d[l], S5_CHUNK)
        ys = _s5(u, tables, bsz, S5_CHUNK)
        merged = _branch(ys, v, gates, w_s5_gate[l].astype(BF16), w_s5_up[l].astype(BF16),
                         w_conv_out[l].astype(BF16), conv_dw[l], conv_dw_b[l], conv_ln_g[l], conv_ln_b[l],
                         seq, tl["branch"])

        w_r = jnp.zeros((d, LANES), F32).at[:, :n_exp].set(w_route_expert[l])
        w_r = w_r.at[:, n_exp:n_exp + n_grp].set(w_route_group[l])
        b_r = jnp.zeros((1, LANES), F32).at[0, :n_exp].set(b_route_expert[l])
        b_r = b_r.at[0, n_exp:n_exp + n_grp].set(b_route_group[l])
        x1, h2, route, route_t, counts = _mix(merged, x2, gate1, ln1_g[l], ln1_b[l], scale2, shift2,
                                              w_out[l].astype(BF16), w_r, b_r, seq, tl["mix"], alpha, n_exp, n_grp)
        row_tok, row_slot, block_expert, n_used, n_valid = _dispatch_tables(route_t, counts, n_exp,
                                                                            d // (2 * LANES))
        y_slots = _moe(h2, row_tok, row_slot, block_expert, n_used, n_valid, w_exp_gate[l], w_exp_up[l],
                       w_exp_down[l], t * TOP_K)
        x2 = _final(y_slots, route, x1, gate2, ln2_g[l], ln2_b[l], seq, tl["final"], alpha)
    return x2.reshape(bsz, seq, d)
```

```python
import functools

import jax
import jax.numpy as jnp
from jax import lax
from jax.experimental import pallas as pl
from jax.experimental.pallas import tpu as pltpu

F32 = jnp.float32
BF16 = jnp.bfloat16
LN_EPS = 1e-5
TOP_K = 2
LANES = 128
S5_CHUNK = 16
CONV_HALO = 32
MOE_ROWS = 256
MOE_DMA_GROUP = 32
VMEM_LIMIT = 56 * 1024 * 1024


def _cparams(*sem):
    return pltpu.CompilerParams(dimension_semantics=sem, vmem_limit_bytes=VMEM_LIMIT)


def _ln(x):
    mu = jnp.mean(x, axis=-1, keepdims=True)
    xc = x - mu
    var = jnp.mean(xc * xc, axis=-1, keepdims=True)
    return xc * lax.rsqrt(var + LN_EPS)


def _dot(a, b):
    return jnp.dot(a, b, preferred_element_type=F32)


def _sigmoid(x):
    return 0.5 * jnp.tanh(0.5 * x) + 0.5


def _pack_pairs(a):
    w = a.shape[1] // 2
    hi = lax.bitcast_convert_type(a[:, :w].astype(BF16).astype(F32), jnp.uint32)
    lo = lax.bitcast_convert_type(a[:, w:].astype(BF16).astype(F32), jnp.uint32)
    return hi | (lo >> 16)


def _unpack_pairs(words):
    hi = lax.bitcast_convert_type(words & jnp.uint32(0xFFFF0000), F32)
    lo = lax.bitcast_convert_type(words << 16, F32)
    return jnp.concatenate([hi, lo], axis=-1)


def _store_rows(ref, first, n, k, packed):
    for j in range(k):
        ref[pl.ds(first + j, n, stride=k), :] = packed[:, j * LANES:(j + 1) * LANES]


def _load_rows(ref, first, n, k, stride):
    return jnp.concatenate([ref[pl.ds(first + j, n, stride=stride), :] for j in range(k)], axis=-1)


def _ada_kernel(c_ref, w_ref, b_ref, o_ref):
    c = c_ref[...]
    o_ref[...] = _dot(c * jax.nn.sigmoid(c), w_ref[...]) + b_ref[...]


def _ada(c, w, b, tn=1024):
    bsz, d = c.shape
    n = w.shape[1]
    rows = 8
    cp = jnp.zeros((rows, d), F32).at[:bsz].set(c)
    out = pl.pallas_call(
        _ada_kernel,
        grid=(n // tn,),
        in_specs=[pl.BlockSpec((rows, d), lambda j: (0, 0)),
                  pl.BlockSpec((d, tn), lambda j: (0, j)),
                  pl.BlockSpec((1, tn), lambda j: (0, j))],
        out_specs=pl.BlockSpec((rows, tn), lambda j: (0, j)),
        out_shape=jax.ShapeDtypeStruct((rows, n), F32),
        compiler_params=_cparams("arbitrary"),
        name="ada",
    )(cp, w, b.reshape(1, n))
    return out[:bsz]


def _proj_kernel(x_ref, shift_ref, scale_ref, w_ref, b_ref, u_ref, v_ref, g_ref, h_scr, a_scr,
                 *, n_u, n_a):
    j = pl.program_id(1)

    @pl.when(j == 0)
    def _():
        h = _ln(x_ref[...]) * (1.0 + scale_ref[0]) + shift_ref[0]
        h_scr[...] = h.astype(BF16)

    p = _dot(h_scr[...], w_ref[...]) + b_ref[...]

    @pl.when(j < n_u)
    def _():
        for q in range(u_ref.shape[0]):
            u_ref[q] = p[:, q * LANES:(q + 1) * LANES]

    @pl.when((j >= n_u) & (j < n_u + n_a))
    def _():
        a_scr[j - n_u] = p

    @pl.when((j >= n_u + n_a) & (j < n_u + 2 * n_a))
    def _():
        v_ref[...] = a_scr[j - n_u - n_a] * _sigmoid(p)

    @pl.when(j >= n_u + 2 * n_a)
    def _():
        g_ref[...] = _sigmoid(p.astype(BF16))


def _proj(x2, shift, scale, w_bf, b, s5_width, conv_ch, seq, tm, tn):
    t, d = x2.shape
    n = w_bf.shape[1]
    n_u, n_a = s5_width // tn, conv_ch // tn
    assert n_u == 1, "the S5 input must be one column tile"
    n_lt = s5_width // LANES
    n_g = (n - s5_width - 2 * conv_ch) // tn
    tiles_per_batch = seq // tm
    g0 = n_u + 2 * n_a
    kern = functools.partial(_proj_kernel, n_u=n_u, n_a=n_a)
    return pl.pallas_call(
        kern,
        grid=(t // tm, n // tn),
        in_specs=[pl.BlockSpec((tm, d), lambda i, j: (i, 0)),
                  pl.BlockSpec((1, 1, d), lambda i, j: (i // tiles_per_batch, 0, 0)),
                  pl.BlockSpec((1, 1, d), lambda i, j: (i // tiles_per_batch, 0, 0)),
                  pl.BlockSpec((d, tn), lambda i, j: (0, j)),
                  pl.BlockSpec((1, tn), lambda i, j: (0, j))],
        out_specs=[pl.BlockSpec((n_lt, tm, LANES), lambda i, j: (0, i, 0)),
                   pl.BlockSpec((tm, tn), lambda i, j: (i, jnp.clip(j - n_u - n_a, 0, n_a - 1))),
                   pl.BlockSpec((tm, tn), lambda i, j: (i, jnp.clip(j - g0, 0, n_g - 1)))],
        out_shape=[jax.ShapeDtypeStruct((n_lt, t, LANES), F32),
                   jax.ShapeDtypeStruct((t, conv_ch), F32),
                   jax.ShapeDtypeStruct((t, n_g * tn), BF16)],
        scratch_shapes=[pltpu.VMEM((tm, d), BF16), pltpu.VMEM((n_a, tm, tn), F32)],
        compiler_params=_cparams("arbitrary", "arbitrary"),
        name="proj",
    )(x2, shift, scale, w_bf, b.reshape(1, n))


def _s5_tables(a_re, a_im, log_dt, b_re, b_im, c_re, c_im, d_skip, chunk):
    hp = lax.Precision.HIGHEST
    g, p = a_re.shape
    hw = b_re.shape[-1]
    w = chunk * hw
    dt = jnp.exp(log_dt.astype(F32))[:, None, None]
    lr, li = a_re.astype(F32)[:, :, None], a_im.astype(F32)[:, :, None]

    def power(k):
        mag = jnp.exp(k * lr * dt)
        return mag * jnp.cos(k * li * dt), mag * jnp.sin(k * li * dt)

    a1_re, a1_im = power(1.0)
    den = lr * lr + li * li
    nr, ni = a1_re - 1.0, a1_im
    z_re = (nr * lr + ni * li) / den
    z_im = (ni * lr - nr * li) / den
    br, bi = b_re.astype(F32), b_im.astype(F32)
    bb_re = z_re * br - z_im * bi
    bb_im = z_re * bi + z_im * br
    lag = (jnp.arange(w + hw) // hw).astype(F32)[None, None, :]
    pw_re, pw_im = power(lag)
    ct_re = jnp.tile(c_re.astype(F32).transpose(0, 2, 1), (1, 1, chunk + 1))
    ct_im = jnp.tile(c_im.astype(F32).transpose(0, 2, 1), (1, 1, chunk + 1))
    ca_re = ct_re * pw_re - ct_im * pw_im
    ca_im = ct_re * pw_im + ct_im * pw_re
    kt = (jnp.einsum('gpj,gpl->gjl', bb_re, ca_re[:, :, :w], precision=hp)
          - jnp.einsum('gpj,gpl->gjl', bb_im, ca_im[:, :, :w], precision=hp))
    m = jnp.stack([jnp.pad(kt[:, :, :w - s * hw], ((0, 0), (0, 0), (s * hw, 0))) for s in range(chunk)],
                  axis=1).reshape(g, w, w)
    rev = (chunk - 1 - jnp.arange(w) // hw).astype(F32)[None, None, :]
    rv_re, rv_im = power(rev)
    bt_re, bt_im = jnp.tile(bb_re, (1, 1, chunk)), jnp.tile(bb_im, (1, 1, chunk))
    pt_re = rv_re * bt_re - rv_im * bt_im
    pt_im = rv_re * bt_im + rv_im * bt_re
    qo_re, qo_im = ca_re[:, :, hw:], -ca_im[:, :, hw:]
    al_re, al_im = power(float(chunk))
    al_re, al_im = al_re.reshape(g, 1, p), al_im.reshape(g, 1, p)
    d_t = jnp.tile(d_skip.astype(F32)[:, None, :], (1, 1, chunk))
    return m, pt_re, pt_im, qo_re, qo_im, al_re, al_im, d_t


def _s5_kernel(u_ref, m_ref, pre_ref, pim_ref, qre_ref, qim_ref, are_ref, aim_ref, d_ref, y_ref,
               zre, zim, sre, sim, *, bsz):
    u = u_ref[0]
    rows = u.shape[0]
    nc = rows // bsz
    y = _dot(u, m_ref[0]) + d_ref[0] * u
    nt = (((1,), (1,)), ((), ()))
    zre[...] = lax.dot_general(u, pre_ref[0], nt, preferred_element_type=F32)
    zim[...] = lax.dot_general(u, pim_ref[0], nt, preferred_element_type=F32)
    ar, ai = are_ref[0], aim_ref[0]
    npair = zre.shape[1]

    def step(c, carry):
        nxt = []
        for b in range(bsz):
            sr, si = carry[2 * b], carry[2 * b + 1]
            row = pl.ds(b * nc + c, 1)
            sre[row, :] = sr
            sim[row, :] = si
            nxt.append(ar * sr - ai * si + zre[row, :])
            nxt.append(ar * si + ai * sr + zim[row, :])
        return tuple(nxt)

    zero = jnp.zeros((1, npair), F32)
    lax.fori_loop(0, nc, step, tuple(zero for _ in range(2 * bsz)))
    y = y + _dot(sre[...], qre_ref[0]) + _dot(sim[...], qim_ref[0])
    y_ref[0] = jax.nn.gelu(y, approximate=True)


def _pack_kernel(u_ref, o_ref, *, chunk):
    g, cb, _ = o_ref.shape
    n_lt = u_ref.shape[0]
    per = g // n_lt
    hw = LANES // per
    for s in range(chunk):
        for q in range(n_lt):
            rows = u_ref[q, pl.ds(s, cb, stride=chunk), :]
            for i in range(per):
                o_ref[q * per + i, :, s * hw:(s + 1) * hw] = rows[:, i * hw:(i + 1) * hw]


def _unpack_kernel(y_ref, o_ref, tmp, *, chunk):
    g, cb, _ = y_ref.shape
    n_lt = o_ref.shape[0]
    per = g // n_lt
    hw = LANES // per
    for s in range(chunk):
        for q in range(n_lt):
            for i in range(per):
                tmp[:, i * hw:(i + 1) * hw] = y_ref[q * per + i, :, s * hw:(s + 1) * hw]
            o_ref[q, pl.ds(s, cb, stride=chunk), :] = tmp[...]


def _s5(u, tables, bsz, chunk):
    m, p_re, p_im, q_re, q_im, a_re, a_im, d_t = tables
    n_lt, t, _ = u.shape
    g, cw, _ = m.shape
    npair = p_re.shape[1]
    rows = t // chunk
    cb = min(64, rows)
    by_chunk = pl.BlockSpec((n_lt, cb * chunk, LANES), lambda i: (0, i, 0))
    by_group = pl.BlockSpec((g, cb, cw), lambda i: (0, i, 0))
    ut = pl.pallas_call(
        functools.partial(_pack_kernel, chunk=chunk), grid=(rows // cb,), in_specs=[by_chunk],
        out_specs=by_group, out_shape=jax.ShapeDtypeStruct((g, rows, cw), F32),
        compiler_params=_cparams("arbitrary"), name="s5_pack",
    )(u)
    gspec = lambda *shape: pl.BlockSpec((1,) + shape, lambda i: (i, 0, 0))
    yt = pl.pallas_call(
        functools.partial(_s5_kernel, bsz=bsz),
        grid=(g,),
        in_specs=[gspec(rows, cw), gspec(cw, cw), gspec(npair, cw), gspec(npair, cw),
                  gspec(npair, cw), gspec(npair, cw), gspec(1, npair), gspec(1, npair), gspec(1, cw)],
        out_specs=gspec(rows, cw),
        out_shape=jax.ShapeDtypeStruct((g, rows, cw), F32),
        scratch_shapes=[pltpu.VMEM((rows, npair), F32) for _ in range(4)],
        compiler_params=_cparams("arbitrary"),
        name="s5",
    )(ut, m, p_re, p_im, q_re, q_im, a_re, a_im, d_t)
    return pl.pallas_call(
        functools.partial(_unpack_kernel, chunk=chunk), grid=(rows // cb,), in_specs=[by_group],
        out_specs=by_chunk, out_shape=jax.ShapeDtypeStruct((n_lt, t, LANES), F32),
        scratch_shapes=[pltpu.VMEM((cb, LANES), F32)],
        compiler_params=_cparams("arbitrary"), name="s5_unpack",
    )(yt)


def _conv_window(win, w_ref, b_ref, g_ref, beta_ref, ts, span):
    sub = 8
    base = CONV_HALO - (span - 1)
    acc = None
    for q in range(sub):
        part = None
        for k in range(span):
            if (base + k) % sub == q:
                lo = base + k - q
                term = win[lo:lo + ts + sub, :] * w_ref[k:k + 1, :]
                part = term if part is None else part + term
        if part is not None:
            part = part[q:q + ts, :]
            acc = part if acc is None else acc + part
    y = _ln(acc + b_ref[...]) * g_ref[...] + beta_ref[...]
    return y * _sigmoid(y)


def _branch_kernel(ys_ref, v0_ref, vn_ref, gs_ref, wgate_ref, wup_ref, wco_ref, wdw_ref, bdw_ref, lng_ref,
                   lnb_ref, o_ref, win, cact, *, span, tiles_per_batch):
    i = pl.program_id(0)
    tm = o_ref.shape[0]
    zeros = functools.partial(jnp.zeros, dtype=F32)

    def conv_tile(v_ref, slot):
        win[CONV_HALO:CONV_HALO + tm, :] = v_ref[...]
        act = _conv_window(win, wdw_ref, bdw_ref, lng_ref, lnb_ref, tm, span)
        cact[slot] = act.astype(BF16)
        win[0:CONV_HALO, :] = win[tm:tm + CONV_HALO, :]

    @pl.when(i == 0)
    def _():
        win[0:CONV_HALO, :] = zeros((CONV_HALO, win.shape[1]))
        win[CONV_HALO + tm:, :] = zeros((win.shape[0] - CONV_HALO - tm, win.shape[1]))
        conv_tile(v0_ref, 0)

    @pl.when((i + 1) % tiles_per_batch == 0)
    def _():
        win[0:CONV_HALO, :] = zeros((CONV_HALO, win.shape[1]))

    conv_tile(vn_ref, (i + 1) % 2)
    ys = jnp.concatenate([ys_ref[q] for q in range(ys_ref.shape[0])], axis=-1)
    glu = ys * _sigmoid(_dot(ys.astype(BF16), wgate_ref[...]))
    y_s5 = _dot(glu.astype(BF16), wup_ref[...])
    y_conv = _dot(cact[i % 2], wco_ref[...])
    d = y_s5.shape[1]
    merged = gs_ref[:, :d].astype(F32) * y_s5 + gs_ref[:, d:].astype(F32) * y_conv
    o_ref[...] = merged.astype(BF16)


def _branch(ys, v, gates, w_gate, w_up, w_co, w_dw, b_dw, ln_g, ln_b, seq, tm):
    n_lt, t, _ = ys.shape
    cc = v.shape[1]
    d = w_up.shape[1]
    span = w_dw.shape[0]
    n_tiles = t // tm
    full = lambda a: pl.BlockSpec(a.shape, lambda i: (0, 0))
    row = lambda a: a.reshape(1, cc)
    w_dw, b_dw, ln_g, ln_b = w_dw.reshape(span, cc), row(b_dw), row(ln_g), row(ln_b)
    kern = functools.partial(_branch_kernel, span=span, tiles_per_batch=seq // tm)
    return pl.pallas_call(
        kern,
        grid=(n_tiles,),
        in_specs=[pl.BlockSpec((n_lt, tm, LANES), lambda i: (0, i, 0)),
                  pl.BlockSpec((tm, cc), lambda i: (0, 0)),
                  pl.BlockSpec((tm, cc), lambda i: (jnp.minimum(i + 1, n_tiles - 1), 0)),
                  pl.BlockSpec((tm, 2 * d), lambda i: (i, 0)),
                  full(w_gate), full(w_up), full(w_co), full(w_dw), full(b_dw), full(ln_g), full(ln_b)],
        out_specs=pl.BlockSpec((tm, d), lambda i: (i, 0)),
        out_shape=jax.ShapeDtypeStruct((t, d), BF16),
        scratch_shapes=[pltpu.VMEM((CONV_HALO + tm + 8, cc), F32), pltpu.VMEM((2, tm, cc), BF16)],
        compiler_params=_cparams("arbitrary"),
        name="branch",
    )(ys, v, v, gates, w_gate, w_up, w_co, w_dw, b_dw, ln_g, ln_b)


def _mix_kernel(m_ref, x_ref, gate_ref, g1_ref, b1_ref, scale2_ref, shift2_ref, wout_ref, wr_ref, br_ref,
                x1_ref, h2_ref, route_ref, routet_ref, cnt_ref, run, *, alpha, n_exp, n_grp):
    @pl.when(pl.program_id(0) == 0)
    def _():
        run[...] = jnp.zeros_like(run)

    mix = _dot(m_ref[...], wout_ref[...])
    x1 = _ln(alpha * x_ref[...] + (1.0 + gate_ref[0]) * mix) * g1_ref[...] + b1_ref[...]
    x1_ref[...] = x1
    h2 = _ln(x1) * (1.0 + scale2_ref[0]) + shift2_ref[0]
    _store_rows(h2_ref, 0, h2.shape[0], h2.shape[1] // (2 * LANES), _pack_pairs(h2))

    logits = _dot(h2, wr_ref[...]) + br_ref[...]
    tm = logits.shape[0]
    per = n_exp // n_grp
    lane = lax.broadcasted_iota(jnp.int32, logits.shape, 1)
    big = jnp.int32(LANES)
    neg = jnp.float32(-jnp.inf)

    def first_max(val):
        top = jnp.max(val, axis=-1, keepdims=True)
        return top, jnp.min(jnp.where(val == top, lane, big), axis=-1, keepdims=True)

    gmask = (lane >= n_exp) & (lane < n_exp + n_grp)
    gtop, glane = first_max(jnp.where(gmask, logits, neg))
    grp = glane - n_exp
    grp_w = 1.0 / jnp.sum(jnp.where(gmask, jnp.exp(logits - gtop), 0.0), axis=-1, keepdims=True)
    emask = (lane >= grp * per) & (lane < grp * per + per)
    el = jnp.where(emask, logits, neg)
    t1, e1 = first_max(el)
    t2, e2 = first_max(jnp.where(lane == e1, neg, el))
    ex = jnp.exp(t2 - t1)
    w1 = grp_w / (1.0 + ex)
    w2 = grp_w * ex / (1.0 + ex)

    oh1 = (lane == e1).astype(F32)
    oh2 = (lane == e2).astype(F32)
    both = oh1 + oh2
    r_i = lax.broadcasted_iota(jnp.int32, (tm, tm), 0)
    c_i = lax.broadcasted_iota(jnp.int32, (tm, tm), 1)
    before = _dot((r_i > c_i).astype(F32), both) + run[...]
    rank1 = jnp.sum(before * oh1, axis=-1, keepdims=True)
    rank2 = jnp.sum(before * oh2, axis=-1, keepdims=True)
    run[...] = run[...] + jnp.sum(both, axis=0, keepdims=True)
    cnt_ref[...] = run[...]

    route = jnp.where(lane == 0, e1.astype(F32), 0.0)
    route = jnp.where(lane == 1, e2.astype(F32), route)
    route = jnp.where(lane == 2, w1, route)
    route = jnp.where(lane == 3, w2, route)
    route = jnp.where(lane == 4, rank1, route)
    route = jnp.where(lane == 5, rank2, route)
    route_ref[...] = route
    routet_ref[...] = route.T[0:routet_ref.shape[0], :]


def _mix(merged, x2, gate1, g1, b1, scale2, shift2, w_out, w_r, b_r, seq, tm, alpha, n_exp, n_grp):
    t, d = x2.shape
    k = d // (2 * LANES)
    tiles_per_batch = seq // tm
    tile = pl.BlockSpec((tm, d), lambda i: (i, 0))
    per_batch = pl.BlockSpec((1, 1, d), lambda i: (i // tiles_per_batch, 0, 0))
    const = lambda a: pl.BlockSpec(a.shape, lambda i: (0, 0))
    g1, b1 = g1.reshape(1, d), b1.reshape(1, d)
    kern = functools.partial(_mix_kernel, alpha=alpha, n_exp=n_exp, n_grp=n_grp)
    return pl.pallas_call(
        kern,
        grid=(t // tm,),
        in_specs=[tile, tile, per_batch, const(g1), const(b1), per_batch, per_batch,
                  const(w_out), const(w_r), const(b_r)],
        out_specs=[tile, pl.BlockSpec((tm * k, LANES), lambda i: (i, 0)),
                   pl.BlockSpec((tm, LANES), lambda i: (i, 0)),
                   pl.BlockSpec((8, tm), lambda i: (0, i)),
                   pl.BlockSpec((1, LANES), lambda i: (0, 0))],
        out_shape=[jax.ShapeDtypeStruct((t, d), F32), jax.ShapeDtypeStruct((t * k, LANES), jnp.uint32),
                   jax.ShapeDtypeStruct((t, LANES), F32), jax.ShapeDtypeStruct((8, t), F32),
                   jax.ShapeDtypeStruct((1, LANES), F32)],
        scratch_shapes=[pltpu.VMEM((1, LANES), F32)],
        compiler_params=_cparams("arbitrary"),
        name="mix_route",
    )(merged, x2, gate1, g1, b1, scale2, shift2, w_out, w_r, b_r)


def _moe_kernel(bexp_ref, nused_ref, nvalid_ref, tok_ref, tok_next_ref, slot_ref, h_hbm, wg_ref, wu_ref, wd_ref,
                y_hbm, xbuf, ybuf, gsem, ssem):
    del bexp_ref
    rows = tok_ref.shape[2]
    k = xbuf.shape[1] // rows
    group = MOE_DMA_GROUP
    b = pl.program_id(0)
    last = pl.num_programs(0) - 1
    n_used = nused_ref[0]
    cur, nxt = b % 2, (b + 1) % 2
    nv_cur = nvalid_ref[b]
    nv_next = nvalid_ref[jnp.minimum(b + 1, last)]

    def per_group(n_valid, fn):
        for g0 in range(0, rows, group):
            pl.when(g0 < n_valid)(functools.partial(fn, g0))

    def gather_start(idx_ref, buf, n_valid):
        def start(g0):
            for r in range(g0, g0 + group):
                src = h_hbm.at[pl.ds(pl.multiple_of(idx_ref[0, 0, r], k), k)]
                pltpu.make_async_copy(src, xbuf.at[buf, pl.ds(r * k, k)], gsem.at[buf]).start()
        per_group(n_valid, start)

    def gather_wait(buf, n_valid):
        def wait(g0):
            pltpu.make_async_copy(h_hbm.at[pl.ds(0, group * k)], xbuf.at[buf, pl.ds(g0 * k, group * k)],
                                  gsem.at[buf]).wait()
        per_group(n_valid, wait)

    def scatter_start(buf, n_valid):
        def start(g0):
            for r in range(g0, g0 + group):
                dst = y_hbm.at[pl.ds(pl.multiple_of(slot_ref[0, 0, r], k), k)]
                pltpu.make_async_copy(ybuf.at[buf, pl.ds(r * k, k)], dst, ssem.at[buf]).start(priority=1)
        per_group(n_valid, start)

    def scatter_wait(buf, n_valid):
        def wait(g0):
            pltpu.make_async_copy(ybuf.at[buf, pl.ds(g0 * k, group * k)], y_hbm.at[pl.ds(0, group * k)],
                                  ssem.at[buf]).wait()
        per_group(n_valid, wait)

    @pl.when(b == 0)
    def _():
        xbuf[...] = jnp.zeros(xbuf.shape, xbuf.dtype)
        ybuf[1] = jnp.zeros(ybuf.shape[1:], ybuf.dtype)
        spare = pltpu.make_async_copy(ybuf.at[1], y_hbm.at[pl.ds(y_hbm.shape[0] - rows * k, rows * k)], ssem.at[1])
        spare.start()
        spare.wait()
        gather_start(tok_ref, 0, nv_cur)

    @pl.when((b >= 2) & (b < n_used))
    def _():
        scatter_wait(cur, nvalid_ref[jnp.maximum(b - 2, 0)])

    @pl.when(b < n_used)
    def _():
        gather_start(tok_next_ref, nxt, nv_next)
        gather_wait(cur, nv_cur)
        x = _unpack_pairs(_load_rows(xbuf.at[cur], 0, rows, k, k))
        hg = _dot(x, wg_ref[0])
        act = hg * _sigmoid(hg) * _dot(x, wu_ref[0])
        _store_rows(ybuf.at[cur], 0, rows, k, _pack_pairs(_dot(act, wd_ref[0])))
        scatter_start(cur, nv_cur)

    @pl.when(b == n_used - 1)
    def _():
        gather_wait(nxt, nv_next)
        scatter_wait(cur, nv_cur)

        @pl.when(b >= 1)
        def _():
            scatter_wait(nxt, nvalid_ref[jnp.maximum(b - 1, 0)])


def _moe(h2, row_tok, row_slot, block_expert, n_used, n_valid, w_g, w_u, w_d, n_slots):
    de, d = w_d.shape[-2:]
    k = d // (2 * LANES)
    n_blocks = row_tok.shape[0]
    smem = lambda off: pl.BlockSpec((1, 1, MOE_ROWS),
                                    lambda b, e, n, v: (jnp.minimum(b + off, n_blocks - 1), 0, 0),
                                    memory_space=pltpu.SMEM)
    grid_spec = pltpu.PrefetchScalarGridSpec(
        num_scalar_prefetch=3,
        grid=(n_blocks,),
        in_specs=[smem(0), smem(1), smem(0),
                  pl.BlockSpec(memory_space=pl.ANY),
                  pl.BlockSpec((1, d, de), lambda b, e, n, v: (e[b], 0, 0)),
                  pl.BlockSpec((1, d, de), lambda b, e, n, v: (e[b], 0, 0)),
                  pl.BlockSpec((1, de, d), lambda b, e, n, v: (e[b], 0, 0))],
        out_specs=pl.BlockSpec(memory_space=pl.ANY),
        scratch_shapes=[pltpu.VMEM((2, MOE_ROWS * k, LANES), jnp.uint32),
                        pltpu.VMEM((2, MOE_ROWS * k, LANES), jnp.uint32),
                        pltpu.SemaphoreType.DMA((2,)), pltpu.SemaphoreType.DMA((2,))],
    )
    return pl.pallas_call(
        _moe_kernel,
        grid_spec=grid_spec,
        out_shape=jax.ShapeDtypeStruct(((n_slots + MOE_ROWS) * k, LANES), jnp.uint32),
        compiler_params=_cparams("arbitrary"),
        name="moe",
    )(block_expert, n_used, n_valid, row_tok, row_tok, row_slot, h2, w_g, w_u, w_d)


def _dispatch_tables(route_t, counts, n_exp, k):
    t = route_t.shape[1]
    n_slots = t * TOP_K
    n_blocks = n_slots // MOE_ROWS + n_exp
    slot_e = route_t[0:TOP_K].astype(jnp.int32)
    rank = route_t[4:4 + TOP_K].astype(jnp.int32)
    cnt = counts[0, :n_exp].astype(jnp.int32)
    padded = (cnt + MOE_ROWS - 1) // MOE_ROWS * MOE_ROWS
    pends = jnp.cumsum(padded)
    pstarts = pends - padded
    experts = jnp.arange(n_exp, dtype=jnp.int32)[:, None, None]
    dest = (jnp.sum(jnp.where(slot_e[None] == experts, pstarts[:, None, None], 0), axis=0) + rank).reshape(-1)
    slot_id = (TOP_K * jnp.arange(t, dtype=jnp.int32)[None, :] + jnp.arange(TOP_K, dtype=jnp.int32)[:, None])
    spare = n_slots + jnp.arange(n_blocks * MOE_ROWS, dtype=jnp.int32) % MOE_ROWS
    row_slot = spare.at[dest].set(slot_id.reshape(-1), unique_indices=True, mode='promise_in_bounds')
    row_tok = jnp.where(row_slot < n_slots, row_slot // TOP_K, 0)
    block_start = jnp.arange(n_blocks, dtype=jnp.int32) * MOE_ROWS
    block_expert = jnp.minimum(jnp.sum(block_start[None, :] >= pends[:, None], axis=0), n_exp - 1).astype(jnp.int32)
    valid_end = jnp.sum(jnp.where(block_expert[None, :] == experts[:, :, 0], (pstarts + cnt)[:, None], 0), axis=0)
    n_valid = jnp.clip(valid_end - block_start, 0, MOE_ROWS).astype(jnp.int32)
    n_used = (pends[-1:] // MOE_ROWS).astype(jnp.int32)
    shape3 = (n_blocks, 1, MOE_ROWS)
    return (row_tok * k).reshape(shape3), (row_slot * k).reshape(shape3), block_expert, n_used, n_valid


def _final_kernel(y_ref, route_ref, x1_ref, gate_ref, g_ref, b_ref, o_ref, *, alpha):
    tm, d = x1_ref.shape
    k = d // (2 * LANES)
    route = route_ref[...]
    y0 = _unpack_pairs(_load_rows(y_ref, 0, tm, k, TOP_K * k))
    y1 = _unpack_pairs(_load_rows(y_ref, k, tm, k, TOP_K * k))
    ffn = route[:, 2:3] * y0 + route[:, 3:4] * y1
    o_ref[...] = _ln(alpha * x1_ref[...] + (1.0 + gate_ref[0]) * ffn) * g_ref[...] + b_ref[...]


def _final(y_slots, route, x1, gate2, g2, b2, seq, tm, alpha):
    t, d = x1.shape
    k = d // (2 * LANES)
    tiles_per_batch = seq // tm
    return pl.pallas_call(
        functools.partial(_final_kernel, alpha=alpha),
        grid=(t // tm,),
        in_specs=[pl.BlockSpec((tm * TOP_K * k, LANES), lambda i: (i, 0)),
                  pl.BlockSpec((tm, LANES), lambda i: (i, 0)),
                  pl.BlockSpec((tm, d), lambda i: (i, 0)),
                  pl.BlockSpec((1, 1, d), lambda i: (i // tiles_per_batch, 0, 0)),
                  pl.BlockSpec((1, d), lambda i: (0, 0)),
                  pl.BlockSpec((1, d), lambda i: (0, 0))],
        out_specs=pl.BlockSpec((tm, d), lambda i: (i, 0)),
        out_shape=jax.ShapeDtypeStruct((t, d), F32),
        compiler_params=_cparams("arbitrary"),
        name="final",
    )(y_slots, route, x1, gate2, g2.reshape(1, d), b2.reshape(1, d))


def _tiles(seq):
    return dict(proj=min(1024, seq), branch=min(512, seq), mix=min(512, seq), final=min(512, seq))


def kernel(x, c, w_ada, b_ada, w_in, b_in, s5_a_re, s5_a_im, s5_log_dt, s5_b_re, s5_b_im, s5_c_re, s5_c_im, s5_d, w_s5_gate, w_s5_up, conv_dw, conv_dw_b, conv_ln_g, conv_ln_b, w_conv_out, w_out, ln1_g, ln1_b, w_route_group, b_route_group, w_route_expert, b_route_expert, w_exp_gate, w_exp_up, w_exp_down, ln2_g, ln2_b):
    bsz, seq, d = x.shape
    t = bsz * seq
    depth = w_ada.shape[0]
    alpha = (2.0 * depth) ** 0.25
    s5_width = w_s5_gate.shape[1]
    conv_ch = conv_dw.shape[-1]
    n_grp = w_route_group.shape[-1]
    n_exp = w_route_expert.shape[-1]
    tl = _tiles(seq)
    tn = min(512, s5_width)
    x2 = x.reshape(t, d)
    for l in range(depth):
        mod = _ada(c, w_ada[l], b_ada[l])
        shift1, scale1, gate1, shift2, scale2, gate2 = [
            m.reshape(bsz, 1, d) for m in jnp.split(mod, 6, axis=-1)]

        u, v, gates = _proj(x2, shift1, scale1, w_in[l].astype(BF16), b_in[l], s5_width, conv_ch, seq,
                            tl["proj"], tn)
        tables = _s5_tables(s5_a_re[l], s5_a_im[l], s5_log_dt[l], s5_b_re[l], s5_b_im[l],
                            s5_c_re[l], s5_c_im[l], s5_d[l], S5_CHUNK)
        ys = _s5(u, tables, bsz, S5_CHUNK)
        merged = _branch(ys, v, gates, w_s5_gate[l].astype(BF16), w_s5_up[l].astype(BF16),
                         w_conv_out[l].astype(BF16), conv_dw[l], conv_dw_b[l], conv_ln_g[l], conv_ln_b[l],
                         seq, tl["branch"])

        w_r = jnp.zeros((d, LANES), F32).at[:, :n_exp].set(w_route_expert[l])
        w_r = w_r.at[:, n_exp:n_exp + n_grp].set(w_route_group[l])
        b_r = jnp.zeros((1, LANES), F32).at[0, :n_exp].set(b_route_expert[l])
        b_r = b_r.at[0, n_exp:n_exp + n_grp].set(b_route_group[l])
        x1, h2, route, route_t, counts = _mix(merged, x2, gate1, ln1_g[l], ln1_b[l], scale2, shift2,
                                              w_out[l].astype(BF16), w_r, b_r, seq, tl["mix"], alpha, n_exp, n_grp)
        row_tok, row_slot, block_expert, n_used, n_valid = _dispatch_tables(route_t, counts, n_exp,
                                                                            d // (2 * LANES))
        y_slots = _moe(h2, row_tok, row_slot, block_expert, n_used, n_valid, w_exp_gate[l], w_exp_up[l],
                       w_exp_down[l], t * TOP_K)
        x2 = _final(y_slots, route, x1, gate2, ln2_g[l], ln2_b[l], seq, tl["final"], alpha)
    return x2.reshape(bsz, seq, d)
```
